```python
import jax, jax.numpy as jnp
from jax import lax
import numpy as np

D_MODEL = 2048
BATCH = 16
SEQ = 2048
DEPTH = 4
DEC_BATCH = 32
DEC_SEQ = 64
PAST_LEN = 2048

CHUNK = 64
N_A_LAYERS = DEPTH // 2
N_B_LAYERS = DEPTH - N_A_LAYERS
D_RNN = D_MODEL
RG_BLOCKS = 8
RG_BLOCK_W = D_RNN // RG_BLOCKS
CONV_W = 4
RG_C = 8.0
N_HEADS = 16
HEAD_DIM = D_MODEL // N_HEADS
LEFT_CHUNKS = 8
LEFT = LEFT_CHUNKS * CHUNK
BAND = LEFT + CHUNK
REL_CLIP = 128
D_FF = ((8 * D_MODEL // 3 + 255) // 256) * 256
EPS = 1e-6

kernel_name = "hawk_yoco_chunk_band_stream_step"


def _rms_norm(x, g):
    xf = x.astype(jnp.float32)
    y = xf * lax.rsqrt(jnp.mean(xf * xf, axis=-1, keepdims=True) + EPS)
    return (y * g.astype(jnp.float32)).astype(x.dtype)


def _modulate(x, shift, scale):
    return x * (1 + scale[:, None, :]) + shift[:, None, :]


def _causal_conv(x, prev, w, b):
    if prev is None:
        prev = jnp.zeros((x.shape[0], CONV_W - 1, x.shape[2]), x.dtype)
    xp = jnp.concatenate([prev.astype(x.dtype), x], axis=1)
    S = x.shape[1]
    y = b + xp[:, 0:S] * w[0]
    for k in range(1, CONV_W):
        y = y + xp[:, k:k + S] * w[k]
    return y, xp[:, -(CONV_W - 1):]


def _lru_combine(left, right):
    a1, b1 = left
    a2, b2 = right
    return a1 * a2, a2 * b1 + b2


def _rg_lru(x, h0, w_a, b_a, w_i, b_i, lam, pos0):
    B, S, C = x.shape
    xb = x.reshape(B, S, RG_BLOCKS, RG_BLOCK_W)
    r = jax.nn.sigmoid(jnp.einsum('bsni,nij->bsnj', xb, w_a).reshape(B, S, C) + b_a)
    i = jax.nn.sigmoid(jnp.einsum('bsni,nij->bsnj', xb, w_i).reshape(B, S, C) + b_i)
    log_a = -RG_C * r.astype(jnp.float32) * jax.nn.softplus(-lam.astype(jnp.float32))
    a = jnp.exp(log_a)
    mult = jnp.sqrt(-jnp.expm1(2.0 * log_a))
    pos = pos0 + jnp.arange(S)
    mult = jnp.where((pos == 0)[None, :, None], 1.0, mult)
    u = mult * (i * x).astype(jnp.float32)
    a_cum, h = lax.associative_scan(_lru_combine, (a, u), axis=1)
    if h0 is not None:
        h = h + a_cum * h0.astype(jnp.float32)[:, None, :]
    return h.astype(x.dtype), h[:, -1]


def _recurrent_block(xn, conv_prev, h0, p, a, pos0):
    gate = jax.nn.gelu(xn @ p['rg_w_gate'][a])
    xb = xn @ p['rg_w_in'][a]
    xc, conv_new = _causal_conv(xb, conv_prev, p['rg_conv_w'][a], p['rg_conv_b'][a])
    h, h_last = _rg_lru(xc, h0, p['rg_w_a'][a], p['rg_b_a'][a], p['rg_w_i'][a], p['rg_b_i'][a],
                        p['rg_lambda'][a], pos0)
    return (h * gate) @ p['rg_w_out'][a], conv_new, h_last


def _band_attention(q, k_new, v_new, k_past, v_past, rel_table):
    B, S, H, Dh = q.shape
    n_chunks = -(-S // CHUNK)
    s_pad = n_chunks * CHUNK
    if k_past is not None:
        k_past = k_past[:, -LEFT:]
        v_past = v_past[:, -LEFT:]
    n_past = 0 if k_past is None else k_past.shape[1]

    def assemble(new, past):
        parts = [jnp.zeros((B, LEFT - n_past, H, Dh), new.dtype)]
        if past is not None:
            parts.append(past.astype(new.dtype))
        parts += [new, jnp.zeros((B, s_pad - S, H, Dh), new.dtype)]
        return jnp.concatenate(parts, axis=1)

    k_full = assemble(k_new, k_past)
    v_full = assemble(v_new, v_past)
    rows = jnp.arange(LEFT + s_pad)
    valid = (rows >= LEFT - n_past) & (rows < LEFT + S)
    q_blocks = jnp.pad(q, ((0, 0), (0, s_pad - S), (0, 0), (0, 0)))
    q_blocks = q_blocks.reshape(B, n_chunks, CHUNK, H, Dh).transpose(1, 0, 2, 3, 4)
    qi = jnp.arange(CHUNK)[:, None]
    kj = jnp.arange(BAND)[None, :]
    rel_idx = jnp.clip(LEFT + qi - kj, -REL_CLIP, REL_CLIP) + REL_CLIP
    bias = rel_table[rel_idx].astype(jnp.float32).transpose(2, 0, 1)
    scale = HEAD_DIM ** -0.5

    def one_chunk(args):
        qb, c = args
        start = c * CHUNK
        kb = lax.dynamic_slice_in_dim(k_full, start, BAND, axis=1)
        vb = lax.dynamic_slice_in_dim(v_full, start, BAND, axis=1)
        vmask = lax.dynamic_slice_in_dim(valid, start, BAND, axis=0)
        s = jnp.einsum('bqhd,bkhd->bhqk', qb, kb).astype(jnp.float32) * scale + bias
        s = jnp.where(vmask, s, -1e30)
        pr = jax.nn.softmax(s, axis=-1).astype(vb.dtype)
        return jnp.einsum('bhqk,bkhd->bqhd', pr, vb)

    out = lax.map(one_chunk, (q_blocks, jnp.arange(n_chunks)))
    return out.transpose(1, 0, 2, 3, 4).reshape(B, s_pad, H, Dh)[:, :S]


def _trunk(x, c, pos0, conv_state, rnn_state, k_cache, v_cache, p):
    B, S, _ = x.shape
    cs = jax.nn.silu(c)
    conv_out, rnn_out = [], []
    k_new = None
    v_new = None
    for layer in range(DEPTH):
        mod = cs @ p['ada_w'][layer] + p['ada_b'][layer]
        sh1, sc1, g1, sh2, sc2, g2 = jnp.split(mod, 6, axis=-1)
        if layer == N_A_LAYERS:
            kvn = _rms_norm(x, p['g_kv'])
            k_new = (kvn @ p['w_k']).reshape(B, S, N_HEADS, HEAD_DIM)
            v_new = (kvn @ p['w_v']).reshape(B, S, N_HEADS, HEAD_DIM)
        hn = _modulate(_rms_norm(x, p['g_mix'][layer]), sh1, sc1)
        if layer < N_A_LAYERS:
            cp = None if conv_state is None else conv_state[layer]
            h0 = None if rnn_state is None else rnn_state[layer]
            out, cst, hst = _recurrent_block(hn, cp, h0, p, layer, pos0)
            conv_out.append(cst)
            rnn_out.append(hst)
        else:
            bl = layer - N_A_LAYERS
            q = (hn @ p['w_q'][bl]).reshape(B, S, N_HEADS, HEAD_DIM)
            o = _band_attention(q, k_new, v_new, k_cache, v_cache, p['rel_bias'][bl])
            out = o.reshape(B, S, N_HEADS * HEAD_DIM) @ p['w_o'][bl]
        x = x + g1[:, None, :] * out
        hf = _modulate(_rms_norm(x, p['g_ffn'][layer]), sh2, sc2)
        ff = (jax.nn.silu(hf @ p['ffn_w1'][layer]) * (hf @ p['ffn_w3'][layer])) @ p['ffn_w2'][layer]
        x = x + g2[:, None, :] * ff
    y = _rms_norm(x, p['g_final'])
    return y, jnp.stack(conv_out), jnp.stack(rnn_out), k_new, v_new


def setup_inputs(seed: int = 0) -> dict:
    key = jax.random.key(seed)
    ks = iter(jax.random.split(key, 40))
    f32 = jnp.float32

    def nrm(shape, scale):
        return jax.random.normal(next(ks), shape, f32) * scale

    D = D_MODEL
    HHD = N_HEADS * HEAD_DIM
    l_cache = min(LEFT, PAST_LEN)
    u = jax.random.uniform(next(ks), (N_A_LAYERS, D_RNN), f32, 0.9, 0.999)
    a0 = u ** (1.0 / RG_C)
    rg_lambda = jnp.log(a0) - jnp.log1p(-a0)
    return {
        'x_prompt': nrm((BATCH, SEQ, D), 1.0),
        'x_sample': nrm((DEC_BATCH, DEC_SEQ, D), 1.0),
        'c_prompt': nrm((BATCH, D), 1.0),
        'c_sample': nrm((DEC_BATCH, D), 1.0),
        'state_conv': nrm((N_A_LAYERS, DEC_BATCH, CONV_W - 1, D_RNN), 1.0),
        'state_rnn': nrm((N_A_LAYERS, DEC_BATCH, D_RNN), 0.5),
        'cache_k': nrm((DEC_BATCH, l_cache, N_HEADS, HEAD_DIM), 1.0),
        'cache_v': nrm((DEC_BATCH, l_cache, N_HEADS, HEAD_DIM), 1.0),
        'ada_w': nrm((DEPTH, D, 6 * D), D ** -0.5),
        'ada_b': nrm((DEPTH, 6 * D), 0.02),
        'g_mix': 1.0 + nrm((DEPTH, D), 0.02),
        'g_ffn': 1.0 + nrm((DEPTH, D), 0.02),
        'rg_w_in': nrm((N_A_LAYERS, D, D_RNN), D ** -0.5),
        'rg_w_gate': nrm((N_A_LAYERS, D, D_RNN), D ** -0.5),
        'rg_conv_w': nrm((N_A_LAYERS, CONV_W, D_RNN), CONV_W ** -0.5),
        'rg_conv_b': nrm((N_A_LAYERS, D_RNN), 0.02),
        'rg_w_a': nrm((N_A_LAYERS, RG_BLOCKS, RG_BLOCK_W, RG_BLOCK_W), RG_BLOCK_W ** -0.5),
        'rg_b_a': nrm((N_A_LAYERS, D_RNN), 0.02),
        'rg_w_i': nrm((N_A_LAYERS, RG_BLOCKS, RG_BLOCK_W, RG_BLOCK_W), RG_BLOCK_W ** -0.5),
        'rg_b_i': nrm((N_A_LAYERS, D_RNN), 0.02),
        'rg_lambda': rg_lambda,
        'rg_w_out': nrm((N_A_LAYERS, D_RNN, D), D_RNN ** -0.5),
        'g_kv': 1.0 + nrm((D,), 0.02),
        'w_k': nrm((D, HHD), D ** -0.5),
        'w_v': nrm((D, HHD), D ** -0.5),
        'w_q': nrm((N_B_LAYERS, D, HHD), D ** -0.5),
        'w_o': nrm((N_B_LAYERS, HHD, D), HHD ** -0.5),
        'rel_bias': nrm((N_B_LAYERS, 2 * REL_CLIP + 1, N_HEADS), 0.5),
        'ffn_w1': nrm((DEPTH, D, D_FF), D ** -0.5),
        'ffn_w3': nrm((DEPTH, D, D_FF), D ** -0.5),
        'ffn_w2': nrm((DEPTH, D_FF, D), D_FF ** -0.5),
        'g_final': 1.0 + nrm((D,), 0.02),
    }


def reference(x_prompt, x_sample, c_prompt, c_sample, state_conv, state_rnn, cache_k, cache_v,
              ada_w, ada_b, g_mix, g_ffn, rg_w_in, rg_w_gate, rg_conv_w, rg_conv_b, rg_w_a, rg_b_a,
              rg_w_i, rg_b_i, rg_lambda, rg_w_out, g_kv, w_k, w_v, w_q, w_o, rel_bias,
              ffn_w1, ffn_w3, ffn_w2, g_final):
    p = {
        'ada_w': ada_w, 'ada_b': ada_b, 'g_mix': g_mix, 'g_ffn': g_ffn,
        'rg_w_in': rg_w_in, 'rg_w_gate': rg_w_gate, 'rg_conv_w': rg_conv_w, 'rg_conv_b': rg_conv_b,
        'rg_w_a': rg_w_a, 'rg_b_a': rg_b_a, 'rg_w_i': rg_w_i, 'rg_b_i': rg_b_i,
        'rg_lambda': rg_lambda, 'rg_w_out': rg_w_out, 'g_kv': g_kv, 'w_k': w_k, 'w_v': w_v,
        'w_q': w_q, 'w_o': w_o, 'rel_bias': rel_bias,
        'ffn_w1': ffn_w1, 'ffn_w3': ffn_w3, 'ffn_w2': ffn_w2, 'g_final': g_final,
    }
    y_prompt, conv_prompt, rnn_prompt, k_p, v_p = _trunk(x_prompt, c_prompt, 0, None, None, None, None, p)
    keep = min(LEFT, x_prompt.shape[1])
    k_prompt = k_p[:, -keep:]
    v_prompt = v_p[:, -keep:]
    y_sample, conv_sample, rnn_sample, k_sample, v_sample = _trunk(
        x_sample, c_sample, PAST_LEN, state_conv, state_rnn, cache_k, cache_v, p)
    return (y_prompt, y_sample, conv_prompt, rnn_prompt, k_prompt, v_prompt,
            conv_sample, rnn_sample, k_sample, v_sample)
```

```python
import functools

import jax
import jax.numpy as jnp
from jax import lax
from jax.experimental import pallas as pl
from jax.experimental.pallas import tpu as pltpu

F32 = jnp.float32
BF16 = jnp.bfloat16

CHUNK = 64
LEFT = 512
BAND = LEFT + CHUNK
REL_CLIP = 128
RG_C = 8.0
RG_BLOCKS = 8
CONV_W = 4
N_HEADS = 16
EPS = 1e-6
PAST_LEN = 2048
NEG = -1e30

LANES = 128
SUBLANES = 8
VMEM_LIMIT = 56 * 1024 * 1024

NORM_ROWS = 128
CONV_PAD = 8


def _cparams(n_axes):
    return pltpu.CompilerParams(
        dimension_semantics=("arbitrary",) * n_axes, vmem_limit_bytes=VMEM_LIMIT)


def _rms(x, g):
    return x * lax.rsqrt(jnp.mean(x * x, axis=-1, keepdims=True) + EPS) * g


def _norm_rows_to(x_ref, g_ref, sh_ref, sc_ref, dst_ref):
    nb, ts, _ = x_ref.shape
    rc = min(ts, NORM_ROWS)
    nr = ts // rc
    g = g_ref[...]

    def body(it, carry):
        b = it // nr
        r = pl.multiple_of((it % nr) * rc, rc)
        y = _rms(x_ref[b, pl.ds(r, rc), :], g)
        if sh_ref is not None:
            y = y * (1.0 + sc_ref[b]) + sh_ref[b]
        dst_ref[pl.ds(pl.multiple_of(it * rc, rc), rc), :] = y.astype(dst_ref.dtype)
        return carry

    lax.fori_loop(0, nb * nr, body, 0)


def _ada_kernel(c_ref, w_ref, b_ref, o_ref):
    c = c_ref[...]
    cs = (c * jax.nn.sigmoid(c)).astype(BF16)
    o_ref[0] = jnp.dot(cs, w_ref[0].astype(BF16), preferred_element_type=F32) + b_ref[0]


def _ada(c, ada_w, ada_b, bn=1024):
    nl, d, n = ada_w.shape
    bt = c.shape[0]
    return pl.pallas_call(
        _ada_kernel,
        grid=(nl, n // bn),
        in_specs=[
            pl.BlockSpec((bt, d), lambda l, j: (0, 0)),
            pl.BlockSpec((1, d, bn), lambda l, j: (l, 0, j)),
            pl.BlockSpec((1, 1, bn), lambda l, j: (l, 0, j)),
        ],
        out_specs=pl.BlockSpec((1, bt, bn), lambda l, j: (l, 0, j)),
        out_shape=jax.ShapeDtypeStruct((nl, bt, n), F32),
        compiler_params=_cparams(2),
        name="ada_mod",
    )(c, ada_w, ada_b.reshape(nl, 1, n))


def _norm_mm_kernel(*refs, use_mod, n_gelu):
    if use_mod:
        x_ref, g_ref, sh_ref, sc_ref, w_ref, o_ref, hn_ref = refs
    else:
        x_ref, g_ref, w_ref, o_ref, hn_ref = refs
        sh_ref = sc_ref = None
    j = pl.program_id(1)

    @pl.when(j == 0)
    def _():
        _norm_rows_to(x_ref, g_ref, sh_ref, sc_ref, hn_ref)

    acc = jnp.dot(hn_ref[...], w_ref[...], preferred_element_type=F32)
    if n_gelu:
        @pl.when(j < n_gelu)
        def _():
            o_ref[...] = jax.nn.gelu(acc).reshape(o_ref.shape).astype(o_ref.dtype)

        @pl.when(j >= n_gelu)
        def _():
            o_ref[...] = acc.reshape(o_ref.shape).astype(o_ref.dtype)
    else:
        o_ref[...] = acc.reshape(o_ref.shape).astype(o_ref.dtype)


def _norm_mm(x, g, shift, scale, w, *, out_dtype, n_gelu_cols=0, bm=512, bn=1024):
    b, s, d = x.shape
    n = w.shape[1]
    ts = min(s, bm)
    nb = bm // ts
    use_mod = shift is not None
    row = lambda i, j: (i // (s // ts), i % (s // ts), 0)
    in_specs = [pl.BlockSpec((nb, ts, d), row),
                pl.BlockSpec((1, d), lambda i, j: (0, 0))]
    args = [x, g.reshape(1, d)]
    if use_mod:
        mod_spec = pl.BlockSpec((nb, 1, d), lambda i, j: (i // (s // ts), 0, 0))
        in_specs += [mod_spec, mod_spec]
        args += [shift, scale]
    in_specs.append(pl.BlockSpec((d, bn), lambda i, j: (0, j)))
    args.append(w)
    return pl.pallas_call(
        functools.partial(_norm_mm_kernel, use_mod=use_mod, n_gelu=n_gelu_cols // bn),
        grid=(b * s // bm, n // bn),
        in_specs=in_specs,
        out_specs=pl.BlockSpec((nb, ts, bn), lambda i, j: (i // (s // ts), i % (s // ts), j)),
        out_shape=jax.ShapeDtypeStruct((b, s, n), out_dtype),
        scratch_shapes=[pltpu.VMEM((bm, d), BF16)],
        compiler_params=_cparams(2),
        name="norm_mm",
    )(*args)


def _proj_res_kernel(a_ref, w_ref, x_ref, g_ref, o_ref):
    nb, ts, k = a_ref.shape
    acc = jnp.dot(a_ref[...].reshape(nb * ts, k), w_ref[...], preferred_element_type=F32)
    o_ref[...] = x_ref[...] + g_ref[...] * acc.reshape(o_ref.shape)


def _proj_res(a, w, x, gate, *, bm=512, bn=1024):
    b, s, d = x.shape
    k = a.shape[2]
    ts = min(s, bm)
    nb = bm // ts
    blk = lambda i, j: (i // (s // ts), i % (s // ts), j)
    return pl.pallas_call(
        _proj_res_kernel,
        grid=(b * s // bm, d // bn),
        in_specs=[
            pl.BlockSpec((nb, ts, k), lambda i, j: (i // (s // ts), i % (s // ts), 0)),
            pl.BlockSpec((k, bn), lambda i, j: (0, j)),
            pl.BlockSpec((nb, ts, bn), blk),
            pl.BlockSpec((nb, 1, bn), lambda i, j: (i // (s // ts), 0, j)),
        ],
        out_specs=pl.BlockSpec((nb, ts, bn), blk),
        out_shape=jax.ShapeDtypeStruct((b, s, d), F32),
        compiler_params=_cparams(2),
        name="proj_res",
    )(a, w, x, gate)


def _ffn_kernel(x_ref, gn_ref, sh_ref, sc_ref, gate_ref, w1_ref, w3_ref, w2_ref, gf_ref,
                o_ref, hf_ref, acc_ref, *, final_norm):
    j = pl.program_id(1)
    nb, ts, _ = x_ref.shape

    @pl.when(j == 0)
    def _():
        _norm_rows_to(x_ref, gn_ref, sh_ref, sc_ref, hf_ref)

    hf = hf_ref[...]
    h1 = jnp.dot(hf, w1_ref[...], preferred_element_type=F32)
    h3 = jnp.dot(hf, w3_ref[...], preferred_element_type=F32)
    act = (h1 * jax.nn.sigmoid(h1) * h3).astype(BF16)
    part = jnp.dot(act, w2_ref[...], preferred_element_type=F32)

    @pl.when(j == 0)
    def _():
        acc_ref[...] = part

    @pl.when(j > 0)
    def _():
        acc_ref[...] += part

    @pl.when(j == pl.num_programs(1) - 1)
    def _():
        rc = min(ts, NORM_ROWS)
        nr = ts // rc
        gf = gf_ref[...]

        def body(it, carry):
            b = it // nr
            r = pl.multiple_of((it % nr) * rc, rc)
            ff = acc_ref[pl.ds(pl.multiple_of(it * rc, rc), rc), :]
            y = x_ref[b, pl.ds(r, rc), :] + gate_ref[b] * ff
            if final_norm:
                y = _rms(y, gf)
            o_ref[b, pl.ds(r, rc), :] = y
            return carry

        lax.fori_loop(0, nb * nr, body, 0)


def _ffn(x, gn, shift, scale, gate, w1, w3, w2, gf, *, final_norm, bm=512, bf=512):
    b, s, d = x.shape
    f = w1.shape[1]
    ts = min(s, bm)
    nb = bm // ts
    row = lambda i, j: (i // (s // ts), i % (s // ts), 0)
    mod_spec = pl.BlockSpec((nb, 1, d), lambda i, j: (i // (s // ts), 0, 0))
    vec_spec = pl.BlockSpec((1, d), lambda i, j: (0, 0))
    return pl.pallas_call(
        functools.partial(_ffn_kernel, final_norm=final_norm),
        grid=(b * s // bm, f // bf),
        in_specs=[
            pl.BlockSpec((nb, ts, d), row), vec_spec, mod_spec, mod_spec, mod_spec,
            pl.BlockSpec((d, bf), lambda i, j: (0, j)),
            pl.BlockSpec((d, bf), lambda i, j: (0, j)),
            pl.BlockSpec((bf, d), lambda i, j: (j, 0)),
            vec_spec,
        ],
        out_specs=pl.BlockSpec((nb, ts, d), row),
        out_shape=jax.ShapeDtypeStruct((b, s, d), F32),
        scratch_shapes=[pltpu.VMEM((bm, d), BF16), pltpu.VMEM((bm, d), F32)],
        compiler_params=_cparams(2),
        name="ffn",
    )(x, gn.reshape(1, d), shift, scale, gate, w1, w3, w2, gf.reshape(1, d))


def _scan8(a, b):
    row = lax.broadcasted_iota(jnp.int32, a.shape, 0)
    for k in (1, 2, 4):
        a_sh = jnp.where(row >= k, pltpu.roll(a, k, axis=0), 1.0)
        b_sh = jnp.where(row >= k, pltpu.roll(b, k, axis=0), 0.0)
        b = a * b_sh + b
        a = a * a_sh
    return a, b


def _rglru_kernel(xb_ref, gate_ref, cprev_ref, h0_ref, cw_ref, cb_ref, wa_ref, wi_ref,
                  ba_ref, bi_ref, lam_ref, hg_ref, cnew_ref, hlast_ref,
                  xpad_ref, a_ref, u_ref, hc_ref, *, pos0, scan_lanes):
    t = pl.program_id(1)
    ts, d = xb_ref.shape[1:]
    bw = d // RG_BLOCKS
    hist = CONV_W - 1
    h_lo = CONV_PAD - hist

    @pl.when(t == 0)
    def _():
        xpad_ref[h_lo:CONV_PAD, :] = cprev_ref[0]
        hc_ref[...] = jnp.broadcast_to(h0_ref[0], hc_ref.shape)

    @pl.when(t > 0)
    def _():
        xpad_ref[h_lo:CONV_PAD, :] = xpad_ref[ts + h_lo:ts + CONV_PAD, :]

    xpad_ref[CONV_PAD:CONV_PAD + ts, :] = xb_ref[0]

    first = (pos0 + t * ts + lax.broadcasted_iota(jnp.int32, (ts, bw), 0)) == 0
    for n in range(RG_BLOCKS):
        cs = slice(n * bw, (n + 1) * bw)
        xc = cb_ref[:, cs] + xpad_ref[h_lo:h_lo + ts, cs] * cw_ref[0:1, cs]
        for k in range(1, CONV_W):
            xc = xc + xpad_ref[h_lo + k:h_lo + k + ts, cs] * cw_ref[k:k + 1, cs]
        xcb = xc.astype(BF16)
        r = jax.nn.sigmoid(jnp.dot(xcb, wa_ref[n], preferred_element_type=F32) + ba_ref[:, cs])
        i = jax.nn.sigmoid(jnp.dot(xcb, wi_ref[n], preferred_element_type=F32) + bi_ref[:, cs])
        lam = lam_ref[:, cs]
        softplus = jnp.maximum(-lam, 0.0) + jnp.log1p(jnp.exp(-jnp.abs(lam)))
        log_a = -RG_C * r * softplus
        a = jnp.exp(log_a)
        mult = jnp.sqrt(-jnp.tanh(log_a) * (a * a + 1.0))
        mult = jnp.where(first, 1.0, mult)
        a_ref[:, cs] = a
        u_ref[:, cs] = mult * (i * xc)

    for c in range(d // scan_lanes):
        cs = slice(c * scan_lanes, (c + 1) * scan_lanes)

        def group(gi, hprev, cs=cs):
            rows = pl.ds(pl.multiple_of(gi * SUBLANES, SUBLANES), SUBLANES)
            a_cum, h_loc = _scan8(a_ref[rows, cs], u_ref[rows, cs])
            h = h_loc + a_cum * hprev
            u_ref[rows, cs] = h
            return jnp.broadcast_to(h[SUBLANES - 1:SUBLANES, :], h.shape)

        hc_ref[:, cs] = lax.fori_loop(0, ts // SUBLANES, group, hc_ref[:, cs], unroll=2)

    hg_ref[0] = (u_ref[...] * gate_ref[0]).astype(hg_ref.dtype)

    @pl.when(t == pl.num_programs(1) - 1)
    def _():
        cnew_ref[0] = xpad_ref[ts + h_lo:ts + CONV_PAD, :]
        hlast_ref[0] = hc_ref[0:1, :]


def _rglru(gx, conv_prev, h0, cw, cb, wa, wi, ba, bi, lam, *, pos0, ts=256, scan_lanes=1024):
    b, s, d2 = gx.shape
    d = d2 // 2
    ts = min(s, ts)
    hist = CONV_W - 1
    vec = lambda a: a.reshape(1, d)
    vec_spec = pl.BlockSpec((1, d), lambda i, t: (0, 0))
    wblk_spec = pl.BlockSpec(wa.shape, lambda i, t: (0, 0, 0))
    hg, cnew, hlast = pl.pallas_call(
        functools.partial(_rglru_kernel, pos0=pos0, scan_lanes=scan_lanes),
        grid=(b, s // ts),
        in_specs=[
            pl.BlockSpec((1, ts, d), lambda i, t: (i, t, 1)),
            pl.BlockSpec((1, ts, d), lambda i, t: (i, t, 0)),
            pl.BlockSpec((1, hist, d), lambda i, t: (i, 0, 0)),
            pl.BlockSpec((1, 1, d), lambda i, t: (i, 0, 0)),
            pl.BlockSpec((CONV_W, d), lambda i, t: (0, 0)),
            vec_spec, wblk_spec, wblk_spec, vec_spec, vec_spec, vec_spec,
        ],
        out_specs=[
            pl.BlockSpec((1, ts, d), lambda i, t: (i, t, 0)),
            pl.BlockSpec((1, hist, d), lambda i, t: (i, 0, 0)),
            pl.BlockSpec((1, 1, d), lambda i, t: (i, 0, 0)),
        ],
        out_shape=[
            jax.ShapeDtypeStruct((b, s, d), BF16),
            jax.ShapeDtypeStruct((b, hist, d), F32),
            jax.ShapeDtypeStruct((b, 1, d), F32),
        ],
        scratch_shapes=[
            pltpu.VMEM((CONV_PAD + ts, d), F32),
            pltpu.VMEM((ts, d), F32),
            pltpu.VMEM((ts, d), F32),
            pltpu.VMEM((SUBLANES, d), F32),
        ],
        compiler_params=_cparams(2),
        name="rglru",
    )(gx, gx, conv_prev, h0, cw, vec(cb), wa, wi, vec(ba), vec(bi), vec(lam))
    return hg, cnew, hlast.reshape(b, d)


def _attn_kernel(*refs, tq, w, n_past, scale):
    if n_past:
        q_ref, k_ref, v_ref, kc_ref, vc_ref, bt_ref, o_ref, bias_ref, kp_ref, vp_ref = refs
    else:
        q_ref, k_ref, v_ref, bt_ref, o_ref, bias_ref, kp_ref, vp_ref = refs
    b = pl.program_id(1)
    t = pl.program_id(2)
    s = k_ref.shape[1]

    @pl.when((b == 0) & (t == 0))
    def _():
        wb = bt_ref.shape[2]
        bias = pltpu.roll(jnp.broadcast_to(bt_ref[0], (tq, wb)), 0, axis=1, stride=1, stride_axis=0)
        qi = lax.broadcasted_iota(jnp.int32, (tq, w), 0)
        kj = lax.broadcasted_iota(jnp.int32, (tq, w), 1)
        lo = (qi // CHUNK) * CHUNK
        bias_ref[...] = jnp.where((kj >= lo) & (kj < lo + BAND), bias[:, :w], NEG)

    @pl.when(t == 0)
    def _():
        if n_past:
            kp_ref[0:LEFT, :] = kc_ref[0].astype(BF16)
            vp_ref[0:LEFT, :] = vc_ref[0].astype(BF16)
        else:
            kp_ref[0:LEFT, :] = jnp.zeros((LEFT, kp_ref.shape[1]), BF16)
            vp_ref[0:LEFT, :] = jnp.zeros((LEFT, vp_ref.shape[1]), BF16)
        kp_ref[LEFT:LEFT + s, :] = k_ref[0].astype(BF16)
        vp_ref[LEFT:LEFT + s, :] = v_ref[0].astype(BF16)
        tail = kp_ref.shape[0] - (LEFT + s)
        if tail:
            kp_ref[LEFT + s:, :] = jnp.zeros((tail, kp_ref.shape[1]), BF16)
            vp_ref[LEFT + s:, :] = jnp.zeros((tail, vp_ref.shape[1]), BF16)

    q0 = pl.multiple_of(t * tq, tq)
    k = kp_ref[pl.ds(q0, w), :]
    v = vp_ref[pl.ds(q0, w), :]
    sc = lax.dot_general(q_ref[0], k, (((1,), (1,)), ((), ())), preferred_element_type=F32)
    sc = sc * scale + bias_ref[...]
    if n_past < LEFT:
        kj = lax.broadcasted_iota(jnp.int32, (tq, w), 1)
        sc = jnp.where(kj >= (LEFT - n_past) - q0, sc, NEG)
    m = jnp.max(sc, axis=-1, keepdims=True)
    p = jnp.exp(sc - m)
    l = jnp.sum(p, axis=-1, keepdims=True)
    o = jnp.dot(p.astype(BF16), v, preferred_element_type=F32) / l
    o_ref[0] = o.astype(o_ref.dtype)


def _bias_row(rel_table, wb):
    far = rel_table[2 * REL_CLIP]
    n_left = LEFT - REL_CLIP
    n_right = wb - n_left - (2 * REL_CLIP + 1)
    row = jnp.concatenate([
        jnp.broadcast_to(far, (n_left,) + far.shape), rel_table[::-1],
        jnp.broadcast_to(far, (n_right,) + far.shape)], axis=0)
    return row.T.reshape(rel_table.shape[1], 1, wb)


def _attention(q, kv, k_cache, v_cache, rel_table, *, tq=256):
    b, s, d = q.shape
    dh = d // N_HEADS
    tq = min(s, tq)
    w = -(-(tq + LEFT) // LANES) * LANES
    wb = max(w, LEFT + REL_CLIP + 2 * LANES)
    n_past = 0 if k_cache is None else k_cache.shape[1]
    assert n_past in (0, LEFT), "the past band buffer is either absent or exactly LEFT rows"
    rows = (s - tq) + w
    qspec = pl.BlockSpec((1, tq, dh), lambda h, i, t: (i, t, h))
    in_specs = [qspec,
                pl.BlockSpec((1, s, dh), lambda h, i, t: (i, 0, h)),
                pl.BlockSpec((1, s, dh), lambda h, i, t: (i, 0, N_HEADS + h))]
    args = [q, kv, kv]
    if n_past:
        cspec = pl.BlockSpec((1, n_past, dh), lambda h, i, t: (i, 0, h))
        in_specs += [cspec, cspec]
        args += [k_cache, v_cache]
    in_specs.append(pl.BlockSpec((1, 1, wb), lambda h, i, t: (h, 0, 0)))
    args.append(_bias_row(rel_table, wb))
    return pl.pallas_call(
        functools.partial(_attn_kernel, tq=tq, w=w, n_past=n_past, scale=dh ** -0.5),
        grid=(N_HEADS, b, s // tq),
        in_specs=in_specs,
        out_specs=qspec,
        out_shape=jax.ShapeDtypeStruct((b, s, d), BF16),
        scratch_shapes=[pltpu.VMEM((tq, w), F32),
                        pltpu.VMEM((rows, dh), BF16),
                        pltpu.VMEM((rows, dh), BF16)],
        compiler_params=_cparams(3),
        name="band_attn",
    )(*args)


def _trunk(x, mod, pos0, conv_state, rnn_state, k_cache, v_cache, p):
    b, s, d = x.shape
    depth = mod.shape[0]
    n_a = p['rg_w_io'].shape[0]
    conv_out, rnn_out = [], []
    kv = None
    for layer in range(depth):
        sh1, sc1, g1, sh2, sc2, g2 = [m.reshape(b, 1, d) for m in jnp.split(mod[layer], 6, axis=-1)]
        if layer == n_a:
            kv = _norm_mm(x, p['g_kv'], None, None, p['w_kv'], out_dtype=F32)
        if layer < n_a:
            gx = _norm_mm(x, p['g_mix'][layer], sh1, sc1, p['rg_w_io'][layer],
                          out_dtype=F32, n_gelu_cols=d)
            cprev = jnp.zeros((b, CONV_W - 1, d), F32) if conv_state is None else conv_state[layer]
            h0 = jnp.zeros((b, 1, d), F32) if rnn_state is None else rnn_state[layer].reshape(b, 1, d)
            mix, cst, hst = _rglru(gx, cprev, h0, p['rg_conv_w'][layer], p['rg_conv_b'][layer],
                                   p['rg_w_a'][layer], p['rg_w_i'][layer], p['rg_b_a'][layer],
                                   p['rg_b_i'][layer], p['rg_lambda'][layer], pos0=pos0)
            conv_out.append(cst)
            rnn_out.append(hst)
            w_out = p['rg_w_out'][layer]
        else:
            bl = layer - n_a
            q = _norm_mm(x, p['g_mix'][layer], sh1, sc1, p['w_q'][bl], out_dtype=BF16)
            mix = _attention(q, kv, k_cache, v_cache, p['rel_bias'][bl])
            w_out = p['w_o'][bl]
        x = _proj_res(mix, w_out, x, g1)
        x = _ffn(x, p['g_ffn'][layer], sh2, sc2, g2, p['ffn_w1'][layer], p['ffn_w3'][layer],
                 p['ffn_w2'][layer], p['g_final'], final_norm=layer == depth - 1)
    k_new = kv[:, :, :d].reshape(b, s, N_HEADS, d // N_HEADS)
    v_new = kv[:, :, d:].reshape(b, s, N_HEADS, d // N_HEADS)
    return x, jnp.stack(conv_out), jnp.stack(rnn_out), k_new, v_new


def kernel(x_prompt, x_sample, c_prompt, c_sample, state_conv, state_rnn, cache_k, cache_v, ada_w, ada_b, g_mix, g_ffn, rg_w_in, rg_w_gate, rg_conv_w, rg_conv_b, rg_w_a, rg_b_a, rg_w_i, rg_b_i, rg_lambda, rg_w_out, g_kv, w_k, w_v, w_q, w_o, rel_bias, ffn_w1, ffn_w3, ffn_w2, g_final):
    d = x_prompt.shape[2]
    p = {
        'g_mix': g_mix, 'g_ffn': g_ffn, 'g_kv': g_kv, 'g_final': g_final,
        'rg_w_io': jnp.concatenate([rg_w_gate, rg_w_in], axis=-1).astype(BF16),
        'rg_conv_w': rg_conv_w, 'rg_conv_b': rg_conv_b,
        'rg_w_a': rg_w_a.astype(BF16), 'rg_b_a': rg_b_a,
        'rg_w_i': rg_w_i.astype(BF16), 'rg_b_i': rg_b_i,
        'rg_lambda': rg_lambda, 'rg_w_out': rg_w_out.astype(BF16),
        'w_kv': jnp.concatenate([w_k, w_v], axis=-1).astype(BF16),
        'w_q': w_q.astype(BF16), 'w_o': w_o.astype(BF16), 'rel_bias': rel_bias,
        'ffn_w1': ffn_w1.astype(BF16), 'ffn_w3': ffn_w3.astype(BF16), 'ffn_w2': ffn_w2.astype(BF16),
    }
    nbp = c_prompt.shape[0]
    mod = _ada(jnp.concatenate([c_prompt, c_sample], axis=0), ada_w, ada_b)

    y_p, conv_p, rnn_p, k_p, v_p = _trunk(x_prompt, mod[:, :nbp], 0, None, None, None, None, p)
    keep = min(LEFT, x_prompt.shape[1])
    nbs, n_past = cache_k.shape[:2]
    y_s, conv_s, rnn_s, k_s, v_s = _trunk(
        x_sample, mod[:, nbp:], PAST_LEN, state_conv, state_rnn,
        cache_k.reshape(nbs, n_past, d), cache_v.reshape(nbs, n_past, d), p)
    return (y_p, y_s, conv_p, rnn_p, k_p[:, -keep:], v_p[:, -keep:], conv_s, rnn_s, k_s, v_s)
```

```python
import functools

import jax
import jax.numpy as jnp
from jax import lax
from jax.experimental import pallas as pl
from jax.experimental.pallas import tpu as pltpu

F32 = jnp.float32
BF16 = jnp.bfloat16

CHUNK = 64
LEFT = 512
BAND = LEFT + CHUNK
REL_CLIP = 128
RG_C = 8.0
RG_BLOCKS = 8
CONV_W = 4
N_HEADS = 16
EPS = 1e-6
PAST_LEN = 2048
NEG = -1e30
LOG2E = 1.4426950408889634

LANES = 128
SUBLANES = 8
VMEM_LIMIT = 56 * 1024 * 1024

BM = 512
NCHUNK = 1024
NORM_ROWS = 128
CONV_PAD = 8
FFN_COLS = 512
SCAN_LANES = 1024
RG_ROWS = 256
ATTN_ROWS = 256
ATTN_HEADS = 4


def _cparams(n_axes):
    return pltpu.CompilerParams(
        dimension_semantics=("arbitrary",) * n_axes, vmem_limit_bytes=VMEM_LIMIT)


def _row_tiles(s):
    ts = min(s, BM)
    return BM // ts, ts, s // ts


def _resident(shape):
    zeros = (0,) * len(shape)
    return pl.BlockSpec(shape, lambda *idx: zeros, pipeline_mode=pl.Buffered(1))


def _norm_inputs(specs, nb, spt):
    args, in_specs = [], []
    for g, sh, sc, _ in specs:
        d = g.shape[-1]
        args.append(g.reshape(1, d))
        in_specs.append(pl.BlockSpec((1, d), lambda *idx: (0, 0)))
        if sh is not None:
            mod_spec = pl.BlockSpec((nb, 1, d), lambda *idx: (idx[0] // spt, 0, 0))
            args += [sh, sc]
            in_specs += [mod_spec, mod_spec]
    return args, in_specs


def _take_norm_refs(refs, mods):
    out, pos = [], 0
    for has_mod in mods:
        if has_mod:
            out.append(refs[pos:pos + 3])
            pos += 3
        else:
            out.append((refs[pos], None, None))
            pos += 1
    return out, refs[pos:]


def _emit_norms(y, b, rows, norm_refs, out_refs):
    yn = y * lax.rsqrt(jnp.mean(y * y, axis=-1, keepdims=True) + EPS)
    for (g_ref, sh_ref, sc_ref), o_ref in zip(norm_refs, out_refs):
        v = yn * g_ref[...]
        if sh_ref is not None:
            v = v * (1.0 + sc_ref[b]) + sh_ref[b]
        o_ref[b, rows, :] = v.astype(o_ref.dtype)


def _norm_out(specs, b, s, d, nb, ts, spt):
    out_specs = [pl.BlockSpec((nb, ts, d), lambda *idx: (idx[0] // spt, idx[0] % spt, 0))
                 for _ in specs]
    out_shape = [jax.ShapeDtypeStruct((b, s, d), spec[3]) for spec in specs]
    return out_specs, out_shape


def _row_loop(nb, ts, body):
    rc = min(ts, NORM_ROWS)
    nr = ts // rc

    def step(it, carry):
        rows = pl.ds(pl.multiple_of((it % nr) * rc, rc), rc)
        flat = pl.ds(pl.multiple_of(it * rc, rc), rc)
        body(it // nr, rows, flat)
        return carry

    lax.fori_loop(0, nb * nr, step, 0)


def _ada_kernel(c_ref, w_ref, b_ref, o_ref):
    c = c_ref[...]
    cs = (c * jax.nn.sigmoid(c)).astype(BF16)
    o_ref[0] = jnp.dot(cs, w_ref[0].astype(BF16), preferred_element_type=F32) + b_ref[0]


def _ada(c, ada_w, ada_b):
    nl, d, n = ada_w.shape
    bt = c.shape[0]
    return pl.pallas_call(
        _ada_kernel,
        grid=(nl, n // NCHUNK),
        in_specs=[
            pl.BlockSpec((bt, d), lambda l, j: (0, 0)),
            pl.BlockSpec((1, d, NCHUNK), lambda l, j: (l, 0, j)),
            pl.BlockSpec((1, 1, NCHUNK), lambda l, j: (l, 0, j)),
        ],
        out_specs=pl.BlockSpec((1, bt, NCHUNK), lambda l, j: (l, 0, j)),
        out_shape=jax.ShapeDtypeStruct((nl, bt, n), F32),
        compiler_params=_cparams(2),
        name="ada_mod",
    )(c, ada_w, ada_b.reshape(nl, 1, n))


def _norm_kernel(x_ref, *refs, mods):
    norm_refs, out_refs = _take_norm_refs(refs, mods)
    nb, ts, _ = x_ref.shape

    def body(b, rows, flat):
        _emit_norms(x_ref[b, rows, :], b, rows, norm_refs, out_refs)

    _row_loop(nb, ts, body)


def _norm_rows(x, specs):
    b, s, d = x.shape
    nb, ts, spt = _row_tiles(s)
    n_args, n_specs = _norm_inputs(specs, nb, spt)
    out_specs, out_shape = _norm_out(specs, b, s, d, nb, ts, spt)
    return pl.pallas_call(
        functools.partial(_norm_kernel, mods=tuple(sp[1] is not None for sp in specs)),
        grid=(b * s // BM,),
        in_specs=[pl.BlockSpec((nb, ts, d), lambda i: (i // spt, i % spt, 0))] + n_specs,
        out_specs=out_specs,
        out_shape=out_shape,
        compiler_params=_cparams(1),
        name="norm_rows",
    )(x, *n_args)


def _mm_kernel(a_ref, *refs, gelu):
    n_w = len(gelu)
    nb, ts, k = a_ref.shape
    a = a_ref[...].reshape(nb * ts, k)
    for w_ref, o_ref, use_gelu in zip(refs[:n_w], refs[n_w:], gelu):
        for c in range(0, w_ref.shape[1], NCHUNK):
            acc = jnp.dot(a, w_ref[:, c:c + NCHUNK], preferred_element_type=F32)
            if use_gelu:
                acc = jax.nn.gelu(acc)
            o_ref[:, :, c:c + NCHUNK] = acc.reshape(nb, ts, NCHUNK).astype(o_ref.dtype)


def _mm(a, ws, *, gelu, out_dtypes):
    b, s, k = a.shape
    nb, ts, spt = _row_tiles(s)
    row = lambda i: (i // spt, i % spt, 0)
    return pl.pallas_call(
        functools.partial(_mm_kernel, gelu=tuple(gelu)),
        grid=(b * s // BM,),
        in_specs=[pl.BlockSpec((nb, ts, k), row)] + [_resident(w.shape) for w in ws],
        out_specs=[pl.BlockSpec((nb, ts, w.shape[1]), row) for w in ws],
        out_shape=[jax.ShapeDtypeStruct((b, s, w.shape[1]), dt) for w, dt in zip(ws, out_dtypes)],
        compiler_params=_cparams(1),
        name="proj",
    )(a, *ws)


def _proj_res_kernel(a_ref, w_ref, x_ref, gate_ref, *refs, mods):
    norm_refs, (o_ref, *hn_refs) = _take_norm_refs(refs, mods)
    nb, ts, k = a_ref.shape
    a = a_ref[...].reshape(nb * ts, k)
    for c in range(0, w_ref.shape[1], NCHUNK):
        cols = slice(c, c + NCHUNK)
        acc = jnp.dot(a, w_ref[:, cols], preferred_element_type=F32)
        o_ref[:, :, cols] = x_ref[:, :, cols] + gate_ref[:, :, cols] * acc.reshape(nb, ts, NCHUNK)
    rc = min(ts, NORM_ROWS)
    for b in range(nb):
        for r in range(0, ts, rc):
            _emit_norms(o_ref[b, r:r + rc, :], b, slice(r, r + rc), norm_refs, hn_refs)


def _proj_res(a, w, x, gate, specs):
    b, s, d = x.shape
    k = a.shape[2]
    nb, ts, spt = _row_tiles(s)
    row = lambda i: (i // spt, i % spt, 0)
    n_args, n_specs = _norm_inputs(specs, nb, spt)
    out_specs, out_shape = _norm_out(specs, b, s, d, nb, ts, spt)
    return pl.pallas_call(
        functools.partial(_proj_res_kernel, mods=tuple(sp[1] is not None for sp in specs)),
        grid=(b * s // BM,),
        in_specs=[pl.BlockSpec((nb, ts, k), row), _resident(w.shape),
                  pl.BlockSpec((nb, ts, d), row),
                  pl.BlockSpec((nb, 1, d), lambda i: (i // spt, 0, 0))] + n_specs,
        out_specs=[pl.BlockSpec((nb, ts, d), row)] + out_specs,
        out_shape=[jax.ShapeDtypeStruct((b, s, d), F32)] + out_shape,
        compiler_params=_cparams(1),
        name="proj_res",
    )(a, w, x, gate, *n_args)


def _ffn_kernel(hf_ref, x_ref, gate_ref, w1_ref, w3_ref, w2_ref, *refs, mods, emit_x):
    norm_refs, (*out_refs, acc_ref) = _take_norm_refs(refs, mods)
    j = pl.program_id(1)
    nb, ts, d = x_ref.shape

    @pl.when(j == 0)
    def _():
        acc_ref[...] = jnp.zeros(acc_ref.shape, F32)

    hf = hf_ref[...].reshape(nb * ts, d)
    h1 = jnp.dot(hf, w1_ref[...], preferred_element_type=F32)
    h3 = jnp.dot(hf, w3_ref[...], preferred_element_type=F32)
    act = (h1 * jax.nn.sigmoid(h1) * h3).astype(BF16)
    acc_ref[...] += jnp.dot(act, w2_ref[...], preferred_element_type=F32)

    @pl.when(j == pl.num_programs(1) - 1)
    def _():
        def body(b, rows, flat):
            y = x_ref[b, rows, :] + gate_ref[b] * acc_ref[flat, :]
            if emit_x:
                out_refs[0][b, rows, :] = y
            _emit_norms(y, b, rows, norm_refs, out_refs[emit_x:])

        _row_loop(nb, ts, body)


def _ffn(hf, x, gate, w1, w3, w2, specs, *, emit_x):
    b, s, d = x.shape
    f = w1.shape[1]
    nb, ts, spt = _row_tiles(s)
    row = lambda i, j: (i // spt, i % spt, 0)
    n_args, n_specs = _norm_inputs(specs, nb, spt)
    out_specs, out_shape = _norm_out(specs, b, s, d, nb, ts, spt)
    if emit_x:
        out_specs = [pl.BlockSpec((nb, ts, d), row)] + out_specs
        out_shape = [jax.ShapeDtypeStruct((b, s, d), F32)] + out_shape
    return pl.pallas_call(
        functools.partial(_ffn_kernel, mods=tuple(sp[1] is not None for sp in specs), emit_x=emit_x),
        grid=(b * s // BM, f // FFN_COLS),
        in_specs=[
            pl.BlockSpec((nb, ts, d), row), pl.BlockSpec((nb, ts, d), row),
            pl.BlockSpec((nb, 1, d), lambda i, j: (i // spt, 0, 0)),
            pl.BlockSpec((d, FFN_COLS), lambda i, j: (0, j)),
            pl.BlockSpec((d, FFN_COLS), lambda i, j: (0, j)),
            pl.BlockSpec((FFN_COLS, d), lambda i, j: (j, 0)),
        ] + n_specs,
        out_specs=out_specs,
        out_shape=out_shape,
        scratch_shapes=[pltpu.VMEM((BM, d), F32)],
        compiler_params=_cparams(2),
        name="ffn",
    )(hf, x, gate, w1, w3, w2, *n_args)


def _scan8(a, b):
    row = lax.broadcasted_iota(jnp.int32, a.shape, 0)
    for k in (1, 2, 4):
        a_sh = jnp.where(row >= k, pltpu.roll(a, k, axis=0), 1.0)
        b_sh = jnp.where(row >= k, pltpu.roll(b, k, axis=0), 0.0)
        b = a * b_sh + b
        a = a * a_sh
    return a, b


def _rglru_kernel(xb_ref, gate_ref, cprev_ref, h0_ref, cw_ref, cb_ref, wa_ref, wi_ref,
                  ba_ref, bi_ref, lam_ref, hg_ref, cnew_ref, hlast_ref,
                  xpad_ref, a_ref, u_ref, hc_ref, *, pos0):
    t = pl.program_id(1)
    ts, d = xb_ref.shape[1:]
    bw = d // RG_BLOCKS
    hist = CONV_W - 1
    h_lo = CONV_PAD - hist

    @pl.when(t == 0)
    def _():
        xpad_ref[h_lo:CONV_PAD, :] = cprev_ref[0]
        hc_ref[...] = jnp.broadcast_to(h0_ref[0], hc_ref.shape)

    @pl.when(t > 0)
    def _():
        xpad_ref[h_lo:CONV_PAD, :] = xpad_ref[ts + h_lo:ts + CONV_PAD, :]

    xpad_ref[CONV_PAD:CONV_PAD + ts, :] = xb_ref[0]

    first = (pos0 + t * ts + lax.broadcasted_iota(jnp.int32, (ts, bw), 0)) == 0
    for n in range(RG_BLOCKS):
        cs = slice(n * bw, (n + 1) * bw)
        xp = xpad_ref[:, cs]
        xc = cb_ref[:, cs] + xp * cw_ref[0:1, cs]
        for k in range(1, CONV_W):
            xc = pltpu.roll(xc, 1, axis=0) + xp * cw_ref[k:k + 1, cs]
        xc = xc[CONV_PAD:, :]
        xcb = xc.astype(BF16)
        tr = jnp.tanh(jnp.dot(xcb, wa_ref[n], preferred_element_type=F32) + 0.5 * ba_ref[:, cs])
        ti = jnp.tanh(jnp.dot(xcb, wi_ref[n], preferred_element_type=F32) + 0.5 * bi_ref[:, cs])
        lam = lam_ref[:, cs]
        softplus = jnp.maximum(-lam, 0.0) + jnp.log1p(jnp.exp(-jnp.abs(lam)))
        log_a = (tr + 1.0) * (-0.5 * RG_C * softplus)
        a = jnp.exp(log_a)
        z = -jnp.tanh(log_a) * (a * a + 1.0)
        mult = jnp.where(z == 0.0, 0.0, z * lax.rsqrt(z))
        if pos0 <= 0:
            mult = jnp.where(first, 1.0, mult)
        a_ref[:, cs] = a
        u_ref[:, cs] = mult * ((0.5 * ti + 0.5) * xc)

    for c in range(0, d, SCAN_LANES):
        cs = slice(c, c + SCAN_LANES)

        def group(gi, hprev, cs=cs):
            rows = pl.ds(pl.multiple_of(gi * SUBLANES, SUBLANES), SUBLANES)
            a_cum, h_loc = _scan8(a_ref[rows, cs], u_ref[rows, cs])
            h = h_loc + a_cum * hprev
            u_ref[rows, cs] = h
            return jnp.broadcast_to(h[SUBLANES - 1:SUBLANES, :], h.shape)

        hc_ref[:, cs] = lax.fori_loop(0, ts // SUBLANES, group, hc_ref[:, cs], unroll=2)

    hg_ref[0] = (u_ref[...] * gate_ref[0]).astype(hg_ref.dtype)

    @pl.when(t == pl.num_programs(1) - 1)
    def _():
        cnew_ref[0] = xpad_ref[ts + h_lo:ts + CONV_PAD, :]
        hlast_ref[0] = hc_ref[0:1, :]


def _rglru(gate, xb, conv_prev, h0, cw, cb, wa, wi, ba, bi, lam, *, pos0):
    b, s, d = xb.shape
    ts = min(s, RG_ROWS)
    hist = CONV_W - 1
    vec = lambda a: a.reshape(1, d)
    tile = pl.BlockSpec((1, ts, d), lambda i, t: (i, t, 0))
    vec_spec = pl.BlockSpec((1, d), lambda i, t: (0, 0))
    hg, cnew, hlast = pl.pallas_call(
        functools.partial(_rglru_kernel, pos0=pos0),
        grid=(b, s // ts),
        in_specs=[
            tile, tile,
            pl.BlockSpec((1, hist, d), lambda i, t: (i, 0, 0)),
            pl.BlockSpec((1, 1, d), lambda i, t: (i, 0, 0)),
            pl.BlockSpec((CONV_W, d), lambda i, t: (0, 0)),
            vec_spec, _resident(wa.shape), _resident(wi.shape), vec_spec, vec_spec, vec_spec,
        ],
        out_specs=[
            tile,
            pl.BlockSpec((1, hist, d), lambda i, t: (i, 0, 0)),
            pl.BlockSpec((1, 1, d), lambda i, t: (i, 0, 0)),
        ],
        out_shape=[
            jax.ShapeDtypeStruct((b, s, d), BF16),
            jax.ShapeDtypeStruct((b, hist, d), F32),
            jax.ShapeDtypeStruct((b, 1, d), F32),
        ],
        scratch_shapes=[
            pltpu.VMEM((CONV_PAD + ts, d), F32),
            pltpu.VMEM((ts, d), F32),
            pltpu.VMEM((ts, d), F32),
            pltpu.VMEM((SUBLANES, d), F32),
        ],
        compiler_params=_cparams(2),
        name="rglru",
    )(xb, gate, conv_prev, h0, cw, vec(cb), wa, wi, vec(ba), vec(bi), vec(lam))
    return hg, cnew, hlast.reshape(b, d)


def _attn_kernel(*refs, tq, w, n_past, hb, nvar, scale):
    if n_past:
        q_ref, k_ref, v_ref, kc_ref, vc_ref, bt_ref, o_ref, bias_ref, kp_ref, vp_ref = refs
    else:
        q_ref, k_ref, v_ref, bt_ref, o_ref, bias_ref, kp_ref, vp_ref = refs
    b = pl.program_id(1)
    t = pl.program_id(2)
    s = k_ref.shape[1]
    rows, dh = kp_ref.shape[1:]

    @pl.when((b == 0) & (t == 0))
    def _():
        wb = bt_ref.shape[2]
        qi = lax.broadcasted_iota(jnp.int32, (tq, w), 0)
        kj = lax.broadcasted_iota(jnp.int32, (tq, w), 1)
        lo = (qi // CHUNK) * CHUNK
        band = (kj >= lo) & (kj < lo + BAND)
        for hh in range(hb):
            bias = pltpu.roll(jnp.broadcast_to(bt_ref[hh], (tq, wb)), 0, axis=1,
                              stride=1, stride_axis=0)[:, :w]
            for v in range(nvar):
                missing = (LEFT - n_past) - v * tq
                ok = band & (kj >= missing) if missing > 0 else band
                bias_ref[hh, v] = jnp.where(ok, bias * LOG2E, NEG)

    @pl.when(t == 0)
    def _():
        for hh in range(hb):
            cs = slice(hh * dh, (hh + 1) * dh)
            if n_past:
                kp_ref[hh, 0:LEFT, :] = kc_ref[0, :, cs].astype(BF16)
                vp_ref[hh, 0:LEFT, :] = vc_ref[0, :, cs].astype(BF16)
            else:
                kp_ref[hh, 0:LEFT, :] = jnp.zeros((LEFT, dh), BF16)
                vp_ref[hh, 0:LEFT, :] = jnp.zeros((LEFT, dh), BF16)
            kp_ref[hh, LEFT:LEFT + s, :] = k_ref[0, :, cs].astype(BF16)
            vp_ref[hh, LEFT:LEFT + s, :] = v_ref[0, :, cs].astype(BF16)
            if rows > LEFT + s:
                kp_ref[hh, LEFT + s:, :] = jnp.zeros((rows - LEFT - s, dh), BF16)
                vp_ref[hh, LEFT + s:, :] = jnp.zeros((rows - LEFT - s, dh), BF16)

    q0 = pl.multiple_of(t * tq, tq)
    var = jnp.minimum(t, nvar - 1)
    for hh in range(hb):
        cs = slice(hh * dh, (hh + 1) * dh)
        k = kp_ref[hh, pl.ds(q0, w), :]
        v = vp_ref[hh, pl.ds(q0, w), :]
        sc = lax.dot_general(q_ref[0, :, cs], k, (((1,), (1,)), ((), ())),
                             preferred_element_type=F32)
        sc = sc * (scale * LOG2E) + bias_ref[hh, var]
        m = jnp.max(sc, axis=-1, keepdims=True)
        p = jnp.exp2(sc - m)
        l = jnp.sum(p, axis=-1, keepdims=True)
        o = jnp.dot(p.astype(BF16), v, preferred_element_type=F32) / l
        o_ref[0, :, cs] = o.astype(o_ref.dtype)


def _bias_row(rel_table, wb):
    far = rel_table[2 * REL_CLIP]
    n_left = LEFT - REL_CLIP
    n_right = wb - n_left - (2 * REL_CLIP + 1)
    row = jnp.concatenate([
        jnp.broadcast_to(far, (n_left,) + far.shape), rel_table[::-1],
        jnp.broadcast_to(far, (n_right,) + far.shape)], axis=0)
    return row.T.reshape(rel_table.shape[1], 1, wb)


def _attention(q, k, v, k_cache, v_cache, rel_table):
    b, s, d = q.shape
    dh = d // N_HEADS
    tq = min(s, ATTN_ROWS)
    hb = ATTN_HEADS if s > tq else N_HEADS
    w = -(-(tq + LEFT) // LANES) * LANES
    wb = max(w, LEFT + REL_CLIP + 2 * LANES)
    n_past = 0 if k_cache is None else k_cache.shape[1]
    assert n_past in (0, LEFT), "the past band buffer is either absent or exactly LEFT rows"
    assert (LEFT - n_past) % tq == 0
    nvar = (LEFT - n_past) // tq + 1
    rows = (s - tq) + w
    qspec = pl.BlockSpec((1, tq, dh * hb), lambda h, i, t: (i, t, h))
    kspec = pl.BlockSpec((1, s, dh * hb), lambda h, i, t: (i, 0, h))
    in_specs = [qspec, kspec, kspec]
    args = [q, k, v]
    if n_past:
        cspec = pl.BlockSpec((1, n_past, dh * hb), lambda h, i, t: (i, 0, h))
        in_specs += [cspec, cspec]
        args += [k_cache, v_cache]
    in_specs.append(pl.BlockSpec((hb, 1, wb), lambda h, i, t: (h, 0, 0)))
    args.append(_bias_row(rel_table, wb))
    return pl.pallas_call(
        functools.partial(_attn_kernel, tq=tq, w=w, n_past=n_past, hb=hb, nvar=nvar,
                          scale=dh ** -0.5),
        grid=(N_HEADS // hb, b, s // tq),
        in_specs=in_specs,
        out_specs=qspec,
        out_shape=jax.ShapeDtypeStruct((b, s, d), BF16),
        scratch_shapes=[pltpu.VMEM((hb, nvar, tq, w), F32),
                        pltpu.VMEM((hb, rows, dh), BF16),
                        pltpu.VMEM((hb, rows, dh), BF16)],
        compiler_params=_cparams(3),
        name="band_attn",
    )(*args)


def _trunk(x, mod, pos0, conv_state, rnn_state, k_cache, v_cache, p):
    b, s, d = x.shape
    depth = mod.shape[0]
    n_a = p['rg_w_in'].shape[0]
    mods = [[m.reshape(b, 1, d) for m in jnp.split(mod[layer], 6, axis=-1)] for layer in range(depth)]
    mix_spec = lambda l: (p['g_mix'][l], mods[l][0], mods[l][1], BF16)
    kv_spec = (p['g_kv'], None, None, BF16)

    def next_specs(l):
        if l == depth:
            return [(p['g_final'], None, None, F32)]
        return [mix_spec(l)] + ([kv_spec] if l == n_a else [])

    normed = _norm_rows(x, next_specs(0))
    conv_out, rnn_out = [], []
    k_new = v_new = None
    for layer in range(depth):
        _, _, g1, sh2, sc2, g2 = mods[layer]
        if layer == n_a:
            k_new, v_new = _mm(normed[1], [p['w_k'], p['w_v']], gelu=(False, False),
                               out_dtypes=(F32, F32))
        if layer < n_a:
            gate, xb = _mm(normed[0], [p['rg_w_gate'][layer], p['rg_w_in'][layer]],
                           gelu=(True, False), out_dtypes=(F32, F32))
            cprev = jnp.zeros((b, CONV_W - 1, d), F32) if conv_state is None else conv_state[layer]
            h0 = jnp.zeros((b, 1, d), F32) if rnn_state is None else rnn_state[layer].reshape(b, 1, d)
            mix, cst, hst = _rglru(gate, xb, cprev, h0, p['rg_conv_w'][layer], p['rg_conv_b'][layer],
                                   p['rg_w_a'][layer], p['rg_w_i'][layer], p['rg_b_a'][layer],
                                   p['rg_b_i'][layer], p['rg_lambda'][layer], pos0=pos0)
            conv_out.append(cst)
            rnn_out.append(hst)
            w_out = p['rg_w_out'][layer]
        else:
            bl = layer - n_a
            q, = _mm(normed[0], [p['w_q'][bl]], gelu=(False,), out_dtypes=(BF16,))
            mix = _attention(q, k_new, v_new, k_cache, v_cache, p['rel_bias'][bl])
            w_out = p['w_o'][bl]
        x, hf = _proj_res(mix, w_out, x, g1, [(p['g_ffn'][layer], sh2, sc2, BF16)])
        last = layer == depth - 1
        outs = _ffn(hf, x, g2, p['ffn_w1'][layer], p['ffn_w3'][layer], p['ffn_w2'][layer],
                    next_specs(layer + 1), emit_x=not last)
        if last:
            x = outs[0]
        else:
            x, normed = outs[0], outs[1:]
    heads = (b, s, N_HEADS, d // N_HEADS)
    return x, jnp.stack(conv_out), jnp.stack(rnn_out), k_new.reshape(heads), v_new.reshape(heads)


def kernel(x_prompt, x_sample, c_prompt, c_sample, state_conv, state_rnn, cache_k, cache_v, ada_w, ada_b, g_mix, g_ffn, rg_w_in, rg_w_gate, rg_conv_w, rg_conv_b, rg_w_a, rg_b_a, rg_w_i, rg_b_i, rg_lambda, rg_w_out, g_kv, w_k, w_v, w_q, w_o, rel_bias, ffn_w1, ffn_w3, ffn_w2, g_final):
    d = x_prompt.shape[2]
    p = {
        'g_mix': g_mix, 'g_ffn': g_ffn, 'g_kv': g_kv, 'g_final': g_final,
        'rg_w_gate': rg_w_gate.astype(BF16), 'rg_w_in': rg_w_in.astype(BF16),
        'rg_conv_w': rg_conv_w, 'rg_conv_b': rg_conv_b,
        'rg_w_a': (0.5 * rg_w_a).astype(BF16), 'rg_b_a': rg_b_a,
        'rg_w_i': (0.5 * rg_w_i).astype(BF16), 'rg_b_i': rg_b_i,
        'rg_lambda': rg_lambda, 'rg_w_out': rg_w_out.astype(BF16),
        'w_k': w_k.astype(BF16), 'w_v': w_v.astype(BF16),
        'w_q': w_q.astype(BF16), 'w_o': w_o.astype(BF16), 'rel_bias': rel_bias,
        'ffn_w1': ffn_w1.astype(BF16), 'ffn_w3': ffn_w3.astype(BF16), 'ffn_w2': ffn_w2.astype(BF16),
    }
    nbp = c_prompt.shape[0]
    mod = _ada(jnp.concatenate([c_prompt, c_sample], axis=0), ada_w, ada_b)

    y_p, conv_p, rnn_p, k_p, v_p = _trunk(x_prompt, mod[:, :nbp], 0, None, None, None, None, p)
    keep = min(LEFT, x_prompt.shape[1])
    nbs, n_past = cache_k.shape[:2]
    y_s, conv_s, rnn_s, k_s, v_s = _trunk(
        x_sample, mod[:, nbp:], PAST_LEN, state_conv, state_rnn,
        cache_k.reshape(nbs, n_past, d), cache_v.reshape(nbs, n_past, d), p)
    return (y_p, y_s, conv_p, rnn_p, k_p[:, -keep:], v_p[:, -keep:], conv_s, rnn_s, k_s, v_s)
```

```python
import functools

import jax
import jax.numpy as jnp
from jax import lax
from jax.experimental import pallas as pl
from jax.experimental.pallas import tpu as pltpu

F32 = jnp.float32
BF16 = jnp.bfloat16

CHUNK = 64
LEFT = 512
BAND = LEFT + CHUNK
REL_CLIP = 128
RG_C = 8.0
RG_BLOCKS = 8
CONV_W = 4
N_HEADS = 16
EPS = 1e-6
PAST_LEN = 2048
NEG = -1e30
LOG2E = 1.4426950408889634

LANES = 128
SUBLANES = 8
VMEM_LIMIT = 56 * 1024 * 1024

BM = 512
NCHUNK = 1024
NORM_ROWS = 128
CONV_PAD = 8
FFN_COLS = 512
SCAN_LANES = 1024
RG_ROWS = 256
ATTN_ROWS = 256
ATTN_HEADS = 8


def _cparams(n_axes):
    return pltpu.CompilerParams(
        dimension_semantics=("arbitrary",) * n_axes, vmem_limit_bytes=VMEM_LIMIT)


def _row_tiles(s):
    ts = min(s, BM)
    return BM // ts, ts, s // ts


def _resident(shape):
    zeros = (0,) * len(shape)
    return pl.BlockSpec(shape, lambda *idx: zeros, pipeline_mode=pl.Buffered(1))


def _norm_inputs(specs, nb, spt):
    args, in_specs = [], []
    for g, sh, sc, _ in specs:
        d = g.shape[-1]
        args.append(g.reshape(1, d))
        in_specs.append(pl.BlockSpec((1, d), lambda *idx: (0, 0)))
        if sh is not None:
            mod_spec = pl.BlockSpec((nb, 1, d), lambda *idx: (idx[0] // spt, 0, 0))
            args += [sh, sc]
            in_specs += [mod_spec, mod_spec]
    return args, in_specs


def _take_norm_refs(refs, mods):
    out, pos = [], 0
    for has_mod in mods:
        if has_mod:
            out.append(refs[pos:pos + 3])
            pos += 3
        else:
            out.append((refs[pos], None, None))
            pos += 1
    return out, refs[pos:]


def _emit_norms(y, b, rows, norm_refs, out_refs):
    yn = y * lax.rsqrt(jnp.mean(y * y, axis=-1, keepdims=True) + EPS)
    for (g_ref, sh_ref, sc_ref), o_ref in zip(norm_refs, out_refs):
        v = yn * g_ref[...]
        if sh_ref is not None:
            v = v * (1.0 + sc_ref[b]) + sh_ref[b]
        o_ref[b, rows, :] = v.astype(o_ref.dtype)


def _norm_out(specs, b, s, d, nb, ts, spt):
    out_specs = [pl.BlockSpec((nb, ts, d), lambda *idx: (idx[0] // spt, idx[0] % spt, 0))
                 for _ in specs]
    out_shape = [jax.ShapeDtypeStruct((b, s, d), spec[3]) for spec in specs]
    return out_specs, out_shape


def _row_loop(nb, ts, body):
    rc = min(ts, NORM_ROWS)
    nr = ts // rc

    def step(it, carry):
        rows = pl.ds(pl.multiple_of((it % nr) * rc, rc), rc)
        flat = pl.ds(pl.multiple_of(it * rc, rc), rc)
        body(it // nr, rows, flat)
        return carry

    lax.fori_loop(0, nb * nr, step, 0)


def _ada_kernel(c_ref, w_ref, b_ref, o_ref):
    c = c_ref[...]
    cs = (c * jax.nn.sigmoid(c)).astype(BF16)
    o_ref[0] = jnp.dot(cs, w_ref[0].astype(BF16), preferred_element_type=F32) + b_ref[0]


def _ada(c, ada_w, ada_b):
    nl, d, n = ada_w.shape
    bt = c.shape[0]
    return pl.pallas_call(
        _ada_kernel,
        grid=(nl, n // NCHUNK),
        in_specs=[
            pl.BlockSpec((bt, d), lambda l, j: (0, 0)),
            pl.BlockSpec((1, d, NCHUNK), lambda l, j: (l, 0, j)),
            pl.BlockSpec((1, 1, NCHUNK), lambda l, j: (l, 0, j)),
        ],
        out_specs=pl.BlockSpec((1, bt, NCHUNK), lambda l, j: (l, 0, j)),
        out_shape=jax.ShapeDtypeStruct((nl, bt, n), F32),
        compiler_params=_cparams(2),
        name="ada_mod",
    )(c, ada_w, ada_b.reshape(nl, 1, n))


def _norm_kernel(x_ref, *refs, mods):
    norm_refs, out_refs = _take_norm_refs(refs, mods)
    nb, ts, _ = x_ref.shape

    def body(b, rows, flat):
        _emit_norms(x_ref[b, rows, :], b, rows, norm_refs, out_refs)

    _row_loop(nb, ts, body)


def _norm_rows(x, specs):
    b, s, d = x.shape
    nb, ts, spt = _row_tiles(s)
    n_args, n_specs = _norm_inputs(specs, nb, spt)
    out_specs, out_shape = _norm_out(specs, b, s, d, nb, ts, spt)
    return pl.pallas_call(
        functools.partial(_norm_kernel, mods=tuple(sp[1] is not None for sp in specs)),
        grid=(b * s // BM,),
        in_specs=[pl.BlockSpec((nb, ts, d), lambda i: (i // spt, i % spt, 0))] + n_specs,
        out_specs=out_specs,
        out_shape=out_shape,
        compiler_params=_cparams(1),
        name="norm_rows",
    )(x, *n_args)


def _mm_kernel(a_ref, *refs, gelu):
    n_w = len(gelu)
    nb, ts, k = a_ref.shape
    a = a_ref[...].reshape(nb * ts, k)
    for w_ref, o_ref, use_gelu in zip(refs[:n_w], refs[n_w:], gelu):
        for c in range(0, w_ref.shape[1], NCHUNK):
            acc = jnp.dot(a, w_ref[:, c:c + NCHUNK], preferred_element_type=F32)
            if use_gelu:
                acc = jax.nn.gelu(acc)
            o_ref[:, :, c:c + NCHUNK] = acc.reshape(nb, ts, NCHUNK).astype(o_ref.dtype)


def _mm(a, ws, *, gelu, out_dtypes):
    b, s, k = a.shape
    nb, ts, spt = _row_tiles(s)
    row = lambda i: (i // spt, i % spt, 0)
    return pl.pallas_call(
        functools.partial(_mm_kernel, gelu=tuple(gelu)),
        grid=(b * s // BM,),
        in_specs=[pl.BlockSpec((nb, ts, k), row)] + [_resident(w.shape) for w in ws],
        out_specs=[pl.BlockSpec((nb, ts, w.shape[1]), row) for w in ws],
        out_shape=[jax.ShapeDtypeStruct((b, s, w.shape[1]), dt) for w, dt in zip(ws, out_dtypes)],
        compiler_params=_cparams(1),
        name="proj",
    )(a, *ws)


def _kv_kernel(a_ref, wk_ref, wv_ref, kb_ref, vb_ref, kt_ref, vt_ref, *, spt):
    nb, ts, kdim = a_ref.shape
    dh = kt_ref.shape[3]

    def run(with_tail):
        a = a_ref[...].reshape(nb * ts, kdim)
        for w_ref, ob_ref, ot_ref in ((wk_ref, kb_ref, kt_ref), (wv_ref, vb_ref, vt_ref)):
            for c in range(0, w_ref.shape[1], NCHUNK):
                acc = jnp.dot(a, w_ref[:, c:c + NCHUNK], preferred_element_type=F32)
                acc = acc.reshape(nb, ts, NCHUNK)
                ob_ref[:, :, c:c + NCHUNK] = acc.astype(ob_ref.dtype)
                if with_tail:
                    for h in range(NCHUNK // dh):
                        ot_ref[:, :, c // dh + h, :] = acc[:, :, h * dh:(h + 1) * dh]

    if spt == 1:
        run(True)
    else:
        is_tail = pl.program_id(0) % spt == spt - 1
        pl.when(is_tail)(functools.partial(run, True))
        pl.when(jnp.logical_not(is_tail))(functools.partial(run, False))


def _kv_proj(a, wk, wv):
    b, s, k = a.shape
    n = wk.shape[1]
    dh = n // N_HEADS
    nb, ts, spt = _row_tiles(s)
    assert ts == min(LEFT, s), "the returned K/V rows are exactly the last row tile of a sequence"
    row = lambda i: (i // spt, i % spt, 0)
    full = pl.BlockSpec((nb, ts, n), row)
    tail = pl.BlockSpec((nb, ts, N_HEADS, dh), lambda i: (i // spt, 0, 0, 0))
    return pl.pallas_call(
        functools.partial(_kv_kernel, spt=spt),
        grid=(b * s // BM,),
        in_specs=[pl.BlockSpec((nb, ts, k), row), _resident(wk.shape), _resident(wv.shape)],
        out_specs=[full, full, tail, tail],
        out_shape=[jax.ShapeDtypeStruct((b, s, n), BF16)] * 2
        + [jax.ShapeDtypeStruct((b, ts, N_HEADS, dh), F32)] * 2,
        compiler_params=_cparams(1),
        name="kv_proj",
    )(a, wk, wv)


def _proj_res_kernel(a_ref, w_ref, x_ref, gate_ref, *refs, mods):
    norm_refs, (o_ref, *hn_refs) = _take_norm_refs(refs, mods)
    nb, ts, k = a_ref.shape
    a = a_ref[...].reshape(nb * ts, k)
    for c in range(0, w_ref.shape[1], NCHUNK):
        cols = slice(c, c + NCHUNK)
        acc = jnp.dot(a, w_ref[:, cols], preferred_element_type=F32)
        o_ref[:, :, cols] = x_ref[:, :, cols] + gate_ref[:, :, cols] * acc.reshape(nb, ts, NCHUNK)
    rc = min(ts, NORM_ROWS)
    for b in range(nb):
        for r in range(0, ts, rc):
            _emit_norms(o_ref[b, r:r + rc, :], b, slice(r, r + rc), norm_refs, hn_refs)


def _proj_res(a, w, x, gate, specs):
    b, s, d = x.shape
    k = a.shape[2]
    nb, ts, spt = _row_tiles(s)
    row = lambda i: (i // spt, i % spt, 0)
    n_args, n_specs = _norm_inputs(specs, nb, spt)
    out_specs, out_shape = _norm_out(specs, b, s, d, nb, ts, spt)
    return pl.pallas_call(
        functools.partial(_proj_res_kernel, mods=tuple(sp[1] is not None for sp in specs)),
        grid=(b * s // BM,),
        in_specs=[pl.BlockSpec((nb, ts, k), row), _resident(w.shape),
                  pl.BlockSpec((nb, ts, d), row),
                  pl.BlockSpec((nb, 1, d), lambda i: (i // spt, 0, 0))] + n_specs,
        out_specs=[pl.BlockSpec((nb, ts, d), row)] + out_specs,
        out_shape=[jax.ShapeDtypeStruct((b, s, d), F32)] + out_shape,
        compiler_params=_cparams(1),
        name="proj_res",
    )(a, w, x, gate, *n_args)


def _ffn_kernel(hf_ref, x_ref, gate_ref, w1_ref, w3_ref, w2_ref, *refs, mods, emit_x):
    norm_refs, (*out_refs, acc_ref) = _take_norm_refs(refs, mods)
    j = pl.program_id(1)
    nb, ts, d = x_ref.shape

    @pl.when((pl.program_id(0) == 0) & (j == 0))
    def _():
        acc_ref[...] = jnp.zeros(acc_ref.shape, F32)

    hf = hf_ref[...].reshape(nb * ts, d)
    h1 = jnp.dot(hf, w1_ref[0], preferred_element_type=F32)
    h3 = jnp.dot(hf, w3_ref[0], preferred_element_type=F32)
    act = (h1 * jax.nn.sigmoid(h1) * h3).astype(BF16)
    acc_ref[...] += jnp.dot(act, w2_ref[...], preferred_element_type=F32)

    @pl.when(j == pl.num_programs(1) - 1)
    def _():
        def body(b, rows, flat):
            ff = acc_ref[flat, :]
            acc_ref[flat, :] = jnp.zeros_like(ff)
            y = x_ref[b, rows, :] + gate_ref[b] * ff
            if emit_x:
                out_refs[0][b, rows, :] = y
            _emit_norms(y, b, rows, norm_refs, out_refs[emit_x:])

        _row_loop(nb, ts, body)


def _ffn(hf, x, gate, w1, w3, w2, specs, *, emit_x):
    b, s, d = x.shape
    nb, ts, spt = _row_tiles(s)
    row = lambda i, j: (i // spt, i % spt, 0)
    n_args, n_specs = _norm_inputs(specs, nb, spt)
    out_specs, out_shape = _norm_out(specs, b, s, d, nb, ts, spt)
    if emit_x:
        out_specs = [pl.BlockSpec((nb, ts, d), row)] + out_specs
        out_shape = [jax.ShapeDtypeStruct((b, s, d), F32)] + out_shape
    return pl.pallas_call(
        functools.partial(_ffn_kernel, mods=tuple(sp[1] is not None for sp in specs), emit_x=emit_x),
        grid=(b * s // BM, w1.shape[0]),
        in_specs=[
            pl.BlockSpec((nb, ts, d), row), pl.BlockSpec((nb, ts, d), row),
            pl.BlockSpec((nb, 1, d), lambda i, j: (i // spt, 0, 0)),
            pl.BlockSpec((1, d, FFN_COLS), lambda i, j: (j, 0, 0)),
            pl.BlockSpec((1, d, FFN_COLS), lambda i, j: (j, 0, 0)),
            pl.BlockSpec((FFN_COLS, d), lambda i, j: (j, 0)),
        ] + n_specs,
        out_specs=out_specs,
        out_shape=out_shape,
        scratch_shapes=[pltpu.VMEM((BM, d), F32)],
        compiler_params=_cparams(2),
        name="ffn",
    )(hf, x, gate, w1, w3, w2, *n_args)


def _scan8(a, b):
    row = lax.broadcasted_iota(jnp.int32, a.shape, 0)
    for k in (1, 2, 4):
        a_sh = jnp.where(row >= k, pltpu.roll(a, k, axis=0), 1.0)
        b_sh = jnp.where(row >= k, pltpu.roll(b, k, axis=0), 0.0)
        b = a * b_sh + b
        a = a * a_sh
    return a, b


def _rglru_kernel(xb_ref, gate_ref, cprev_ref, h0_ref, cw_ref, cb_ref, wa_ref, wi_ref,
                  ba_ref, bi_ref, lam_ref, hg_ref, cnew_ref, hlast_ref,
                  xpad_ref, a_ref, u_ref, hc_ref, *, pos0):
    t = pl.program_id(1)
    ts, d = xb_ref.shape[1:]
    bw = d // RG_BLOCKS
    hist = CONV_W - 1
    h_lo = CONV_PAD - hist

    @pl.when(t == 0)
    def _():
        xpad_ref[h_lo:CONV_PAD, :] = cprev_ref[0]
        hc_ref[...] = jnp.broadcast_to(h0_ref[0], hc_ref.shape)

    @pl.when(t > 0)
    def _():
        xpad_ref[h_lo:CONV_PAD, :] = xpad_ref[ts + h_lo:ts + CONV_PAD, :]

    xpad_ref[CONV_PAD:CONV_PAD + ts, :] = xb_ref[0]

    first = (pos0 + t * ts + lax.broadcasted_iota(jnp.int32, (ts, bw), 0)) == 0
    for n in range(RG_BLOCKS):
        cs = slice(n * bw, (n + 1) * bw)
        xp = xpad_ref[:, cs]
        xc = cb_ref[:, cs] + xp * cw_ref[0:1, cs]
        for k in range(1, CONV_W):
            xc = pltpu.roll(xc, 1, axis=0) + xp * cw_ref[k:k + 1, cs]
        xc = xc[CONV_PAD:, :]
        xcb = xc.astype(BF16)
        tr = jnp.tanh(jnp.dot(xcb, wa_ref[n], preferred_element_type=F32) + 0.5 * ba_ref[:, cs])
        ti = jnp.tanh(jnp.dot(xcb, wi_ref[n], preferred_element_type=F32) + 0.5 * bi_ref[:, cs])
        lam = lam_ref[:, cs]
        softplus = jnp.maximum(-lam, 0.0) + jnp.log1p(jnp.exp(-jnp.abs(lam)))
        log_a = (tr + 1.0) * (-0.5 * RG_C * softplus)
        a = jnp.exp(log_a)
        z = -jnp.tanh(log_a) * (a * a + 1.0)
        mult = jnp.where(z == 0.0, 0.0, z * lax.rsqrt(z))
        if pos0 <= 0:
            mult = jnp.where(first, 1.0, mult)
        a_ref[:, cs] = a
        u_ref[:, cs] = mult * ((0.5 * ti + 0.5) * xc)

    for c in range(0, d, SCAN_LANES):
        cs = slice(c, c + SCAN_LANES)

        def group(gi, hprev, cs=cs):
            rows = pl.ds(pl.multiple_of(gi * SUBLANES, SUBLANES), SUBLANES)
            a_cum, h_loc = _scan8(a_ref[rows, cs], u_ref[rows, cs])
            h = h_loc + a_cum * hprev
            u_ref[rows, cs] = h
            return jnp.broadcast_to(h[SUBLANES - 1:SUBLANES, :], h.shape)

        hc_ref[:, cs] = lax.fori_loop(0, ts // SUBLANES, group, hc_ref[:, cs], unroll=2)

    hg_ref[0] = (u_ref[...] * gate_ref[0]).astype(hg_ref.dtype)

    @pl.when(t == pl.num_programs(1) - 1)
    def _():
        cnew_ref[0] = xpad_ref[ts + h_lo:ts + CONV_PAD, :]
        hlast_ref[0] = hc_ref[0:1, :]


def _rglru(gate, xb, conv_prev, h0, cw, cb, wa, wi, ba, bi, lam, *, pos0):
    b, s, d = xb.shape
    ts = min(s, RG_ROWS)
    hist = CONV_W - 1
    vec = lambda a: a.reshape(1, d)
    tile = pl.BlockSpec((1, ts, d), lambda i, t: (i, t, 0))
    vec_spec = pl.BlockSpec((1, d), lambda i, t: (0, 0))
    hg, cnew, hlast = pl.pallas_call(
        functools.partial(_rglru_kernel, pos0=pos0),
        grid=(b, s // ts),
        in_specs=[
            tile, tile,
            pl.BlockSpec((1, hist, d), lambda i, t: (i, 0, 0)),
            pl.BlockSpec((1, 1, d), lambda i, t: (i, 0, 0)),
            pl.BlockSpec((CONV_W, d), lambda i, t: (0, 0)),
            vec_spec, _resident(wa.shape), _resident(wi.shape), vec_spec, vec_spec, vec_spec,
        ],
        out_specs=[
            tile,
            pl.BlockSpec((1, hist, d), lambda i, t: (i, 0, 0)),
            pl.BlockSpec((1, 1, d), lambda i, t: (i, 0, 0)),
        ],
        out_shape=[
            jax.ShapeDtypeStruct((b, s, d), BF16),
            jax.ShapeDtypeStruct((b, hist, d), F32),
            jax.ShapeDtypeStruct((b, 1, d), F32),
        ],
        scratch_shapes=[
            pltpu.VMEM((CONV_PAD + ts, d), F32),
            pltpu.VMEM((ts, d), F32),
            pltpu.VMEM((ts, d), F32),
            pltpu.VMEM((SUBLANES, d), F32),
        ],
        compiler_params=_cparams(2),
        name="rglru",
    )(xb, gate, conv_prev, h0, cw, vec(cb), wa, wi, vec(ba), vec(bi), vec(lam))
    return hg, cnew, hlast.reshape(b, d)


def _build_bias(bt_ref, bias_ref):
    hb, tq, w = bias_ref.shape
    wb = bt_ref.shape[2]
    qi = lax.broadcasted_iota(jnp.int32, (tq, w), 0)
    kj = lax.broadcasted_iota(jnp.int32, (tq, w), 1)
    lo = (qi // CHUNK) * CHUNK
    band = (kj >= lo) & (kj < lo + BAND)
    for hh in range(hb):
        bias = pltpu.roll(jnp.broadcast_to(bt_ref[hh], (tq, wb)), 0, axis=1,
                          stride=1, stride_axis=0)[:, :w]
        bias_ref[hh] = jnp.where(band, bias * LOG2E, NEG)


def _softmax_pv(q, k, v, bias, scale):
    sc = lax.dot_general(q, k, (((1,), (1,)), ((), ())), preferred_element_type=F32)
    sc = sc * (scale * LOG2E) + bias
    m = jnp.max(sc, axis=-1, keepdims=True)
    p = jnp.exp2(sc - m)
    l = jnp.sum(p, axis=-1, keepdims=True)
    return jnp.dot(p.astype(BF16), v, preferred_element_type=F32) / l


def _attn_direct_kernel(q_ref, k_ref, v_ref, bt_ref, o_ref, bias_ref, *, scale):
    b = pl.program_id(1)
    t = pl.program_id(2)
    hb, tq, w = bias_ref.shape
    dh = q_ref.shape[2] // hb

    @pl.when((b == 0) & (t == 0))
    def _():
        _build_bias(bt_ref, bias_ref)

    def tile(row0, wp):
        for hh in range(hb):
            cs = slice(hh * dh, (hh + 1) * dh)
            o = _softmax_pv(q_ref[0, :, cs], k_ref[0, pl.ds(row0, wp), cs],
                            v_ref[0, pl.ds(row0, wp), cs], bias_ref[hh, :, w - wp:], scale)
            o_ref[0, :, cs] = o.astype(o_ref.dtype)

    n_early = LEFT // tq
    for e in range(n_early):
        pl.when(t == e)(functools.partial(tile, 0, (e + 1) * tq))

    @pl.when(t >= n_early)
    def _():
        tile(pl.multiple_of(t * tq - LEFT, tq), w)


def _attn_staged_kernel(q_ref, k_ref, v_ref, kc_ref, vc_ref, bt_ref, o_ref,
                        bias_ref, kp_ref, vp_ref, *, scale):
    b = pl.program_id(1)
    t = pl.program_id(2)
    hb, tq, w = bias_ref.shape
    s = k_ref.shape[1]
    rows, dh = kp_ref.shape[1:]

    @pl.when((b == 0) & (t == 0))
    def _():
        _build_bias(bt_ref, bias_ref)

    @pl.when(t == 0)
    def _():
        for hh in range(hb):
            cs = slice(hh * dh, (hh + 1) * dh)
            kp_ref[hh, 0:LEFT, :] = kc_ref[0, :, cs].astype(BF16)
            vp_ref[hh, 0:LEFT, :] = vc_ref[0, :, cs].astype(BF16)
            kp_ref[hh, LEFT:LEFT + s, :] = k_ref[0, :, cs]
            vp_ref[hh, LEFT:LEFT + s, :] = v_ref[0, :, cs]
            if rows > LEFT + s:
                kp_ref[hh, LEFT + s:, :] = jnp.zeros((rows - LEFT - s, dh), BF16)
                vp_ref[hh, LEFT + s:, :] = jnp.zeros((rows - LEFT - s, dh), BF16)

    q0 = pl.multiple_of(t * tq, tq)
    for hh in range(hb):
        cs = slice(hh * dh, (hh + 1) * dh)
        o = _softmax_pv(q_ref[0, :, cs], kp_ref[hh, pl.ds(q0, w), :], vp_ref[hh, pl.ds(q0, w), :],
                        bias_ref[hh], scale)
        o_ref[0, :, cs] = o.astype(o_ref.dtype)


def _bias_row(rel_table, wb):
    far = rel_table[2 * REL_CLIP]
    n_left = LEFT - REL_CLIP
    n_right = wb - n_left - (2 * REL_CLIP + 1)
    row = jnp.concatenate([
        jnp.broadcast_to(far, (n_left,) + far.shape), rel_table[::-1],
        jnp.broadcast_to(far, (n_right,) + far.shape)], axis=0)
    return row.T.reshape(rel_table.shape[1], 1, wb)


def _attention(q, k, v, k_cache, v_cache, rel_table):
    b, s, d = q.shape
    dh = d // N_HEADS
    tq = min(s, ATTN_ROWS)
    hb = ATTN_HEADS if s > tq else N_HEADS
    w = -(-(tq + LEFT) // LANES) * LANES
    wb = max(w, LEFT + REL_CLIP + 2 * LANES)
    qspec = pl.BlockSpec((1, tq, dh * hb), lambda h, i, t: (i, t, h))
    kspec = pl.BlockSpec((1, s, dh * hb), lambda h, i, t: (i, 0, h))
    bspec = pl.BlockSpec((hb, 1, wb), lambda h, i, t: (h, 0, 0))
    common = dict(
        grid=(N_HEADS // hb, b, s // tq),
        out_specs=qspec,
        out_shape=jax.ShapeDtypeStruct((b, s, d), BF16),
        compiler_params=_cparams(3),
    )
    bias_scratch = pltpu.VMEM((hb, tq, w), F32)
    bias_rows = _bias_row(rel_table, wb)
    if k_cache is None:
        assert LEFT % tq == 0 and tq % LANES == 0 and s % tq == 0
        return pl.pallas_call(
            functools.partial(_attn_direct_kernel, scale=dh ** -0.5),
            in_specs=[qspec, kspec, kspec, bspec],
            scratch_shapes=[bias_scratch],
            name="band_attn", **common,
        )(q, k, v, bias_rows)
    assert k_cache.shape[1] == LEFT, "the past band buffer holds exactly LEFT rows"
    rows = (s - tq) + w
    cspec = pl.BlockSpec((1, LEFT, dh * hb), lambda h, i, t: (i, 0, h))
    return pl.pallas_call(
        functools.partial(_attn_staged_kernel, scale=dh ** -0.5),
        in_specs=[qspec, kspec, kspec, cspec, cspec, bspec],
        scratch_shapes=[bias_scratch, pltpu.VMEM((hb, rows, dh), BF16),
                        pltpu.VMEM((hb, rows, dh), BF16)],
        name="band_attn_past", **common,
    )(q, k, v, k_cache, v_cache, bias_rows)


def _trunk(x, mod, pos0, conv_state, rnn_state, k_cache, v_cache, p):
    b, s, d = x.shape
    depth = mod.shape[0]
    n_a = p['rg_w_in'].shape[0]
    mods = [[m.reshape(b, 1, d) for m in jnp.split(mod[layer], 6, axis=-1)] for layer in range(depth)]
    mix_spec = lambda l: (p['g_mix'][l], mods[l][0], mods[l][1], BF16)
    kv_spec = (p['g_kv'], None, None, BF16)

    def next_specs(l):
        if l == depth:
            return [(p['g_final'], None, None, F32)]
        return [mix_spec(l)] + ([kv_spec] if l == n_a else [])

    normed = _norm_rows(x, next_specs(0))
    conv_out, rnn_out = [], []
    k_new = v_new = k_tail = v_tail = None
    for layer in range(depth):
        _, _, g1, sh2, sc2, g2 = mods[layer]
        if layer == n_a:
            k_new, v_new, k_tail, v_tail = _kv_proj(normed[1], p['w_k'], p['w_v'])
        if layer < n_a:
            gate, xb = _mm(normed[0], [p['rg_w_gate'][layer], p['rg_w_in'][layer]],
                           gelu=(True, False), out_dtypes=(F32, F32))
            cprev = jnp.zeros((b, CONV_W - 1, d), F32) if conv_state is None else conv_state[layer]
            h0 = jnp.zeros((b, 1, d), F32) if rnn_state is None else rnn_state[layer].reshape(b, 1, d)
            mix, cst, hst = _rglru(gate, xb, cprev, h0, p['rg_conv_w'][layer], p['rg_conv_b'][layer],
                                   p['rg_w_a'][layer], p['rg_w_i'][layer], p['rg_b_a'][layer],
                                   p['rg_b_i'][layer], p['rg_lambda'][layer], pos0=pos0)
            conv_out.append(cst)
            rnn_out.append(hst)
            w_out = p['rg_w_out'][layer]
        else:
            bl = layer - n_a
            q, = _mm(normed[0], [p['w_q'][bl]], gelu=(False,), out_dtypes=(BF16,))
            mix = _attention(q, k_new, v_new, k_cache, v_cache, p['rel_bias'][bl])
            w_out = p['w_o'][bl]
        x, hf = _proj_res(mix, w_out, x, g1, [(p['g_ffn'][layer], sh2, sc2, BF16)])
        last = layer == depth - 1
        outs = _ffn(hf, x, g2, p['ffn_w1'][layer], p['ffn_w3'][layer], p['ffn_w2'][layer],
                    next_specs(layer + 1), emit_x=not last)
        if last:
            x = outs[0]
        else:
            x, normed = outs[0], outs[1:]
    return x, jnp.stack(conv_out), jnp.stack(rnn_out), k_tail, v_tail


def _column_blocks(w):
    nl, d, f = w.shape
    return w.reshape(nl, d, f // FFN_COLS, FFN_COLS).transpose(0, 2, 1, 3).astype(BF16)


def kernel(x_prompt, x_sample, c_prompt, c_sample, state_conv, state_rnn, cache_k, cache_v, ada_w, ada_b, g_mix, g_ffn, rg_w_in, rg_w_gate, rg_conv_w, rg_conv_b, rg_w_a, rg_b_a, rg_w_i, rg_b_i, rg_lambda, rg_w_out, g_kv, w_k, w_v, w_q, w_o, rel_bias, ffn_w1, ffn_w3, ffn_w2, g_final):
    d = x_prompt.shape[2]
    p = {
        'g_mix': g_mix, 'g_ffn': g_ffn, 'g_kv': g_kv, 'g_final': g_final,
        'rg_w_gate': rg_w_gate.astype(BF16), 'rg_w_in': rg_w_in.astype(BF16),
        'rg_conv_w': rg_conv_w, 'rg_conv_b': rg_conv_b,
        'rg_w_a': (0.5 * rg_w_a).astype(BF16), 'rg_b_a': rg_b_a,
        'rg_w_i': (0.5 * rg_w_i).astype(BF16), 'rg_b_i': rg_b_i,
        'rg_lambda': rg_lambda, 'rg_w_out': rg_w_out.astype(BF16),
        'w_k': w_k.astype(BF16), 'w_v': w_v.astype(BF16),
        'w_q': w_q.astype(BF16), 'w_o': w_o.astype(BF16), 'rel_bias': rel_bias,
        'ffn_w1': _column_blocks(ffn_w1), 'ffn_w3': _column_blocks(ffn_w3),
        'ffn_w2': ffn_w2.astype(BF16),
    }
    nbp = c_prompt.shape[0]
    mod = _ada(jnp.concatenate([c_prompt, c_sample], axis=0), ada_w, ada_b)

    y_p, conv_p, rnn_p, k_p, v_p = _trunk(x_prompt, mod[:, :nbp], 0, None, None, None, None, p)
    nbs, n_past = cache_k.shape[:2]
    y_s, conv_s, rnn_s, k_s, v_s = _trunk(
        x_sample, mod[:, nbp:], PAST_LEN, state_conv, state_rnn,
        cache_k.reshape(nbs, n_past, d), cache_v.reshape(nbs, n_past, d), p)
    return (y_p, y_s, conv_p, rnn_p, k_p, v_p, conv_s, rnn_s, k_s, v_s)
```

```python
import functools

import jax
import jax.numpy as jnp
from jax import lax
from jax.experimental import pallas as pl
from jax.experimental.pallas import tpu as pltpu

F32 = jnp.float32
BF16 = jnp.bfloat16

CHUNK = 64
LEFT = 512
BAND = LEFT + CHUNK
REL_CLIP = 128
RG_C = 8.0
RG_BLOCKS = 8
CONV_W = 4
N_HEADS = 16
EPS = 1e-6
PAST_LEN = 2048
NEG = -1e30
LOG2E = 1.4426950408889634

LANES = 128
SUBLANES = 8
VMEM_LIMIT = 56 * 1024 * 1024

BM = 512
NCHUNK = 1024
NORM_ROWS = 128
CONV_PAD = 8
FFN_COLS = 512
SCAN_LANES = 1024
RG_ROWS = 256
ATTN_ROWS = 256
ATTN_HEADS = 8


def _cparams(n_axes):
    return pltpu.CompilerParams(
        dimension_semantics=("arbitrary",) * n_axes, vmem_limit_bytes=VMEM_LIMIT)


def _row_tiles(s):
    ts = min(s, BM)
    return BM // ts, ts, s // ts


def _resident(w, layer=None):
    if layer is None:
        shape, index = w.shape, (0,) * w.ndim
    else:
        shape, index = (None,) + w.shape[1:], (layer,) + (0,) * (w.ndim - 1)
    return pl.BlockSpec(shape, lambda *idx: index, pipeline_mode=pl.Buffered(1))


def _norm_inputs(specs, nb, spt):
    args, in_specs = [], []
    for g, sh, sc, _ in specs:
        d = g.shape[-1]
        args.append(g.reshape(1, d))
        in_specs.append(pl.BlockSpec((1, d), lambda *idx: (0, 0)))
        if sh is not None:
            mod_spec = pl.BlockSpec((nb, 1, d), lambda *idx: (idx[0] // spt, 0, 0))
            args += [sh, sc]
            in_specs += [mod_spec, mod_spec]
    return args, in_specs


def _take_norm_refs(refs, mods):
    out, pos = [], 0
    for has_mod in mods:
        if has_mod:
            out.append(refs[pos:pos + 3])
            pos += 3
        else:
            out.append((refs[pos], None, None))
            pos += 1
    return out, refs[pos:]


def _emit_norms(y, b, rows, norm_refs, out_refs):
    yn = y * lax.rsqrt(jnp.mean(y * y, axis=-1, keepdims=True) + EPS)
    for (g_ref, sh_ref, sc_ref), o_ref in zip(norm_refs, out_refs):
        v = yn * g_ref[...]
        if sh_ref is not None:
            v = v * (1.0 + sc_ref[b]) + sh_ref[b]
        o_ref[b, rows, :] = v.astype(o_ref.dtype)


def _norm_out(specs, b, s, d, nb, ts, spt):
    out_specs = [pl.BlockSpec((nb, ts, d), lambda *idx: (idx[0] // spt, idx[0] % spt, 0))
                 for _ in specs]
    out_shape = [jax.ShapeDtypeStruct((b, s, d), spec[3]) for spec in specs]
    return out_specs, out_shape


def _row_loop(nb, ts, body):
    rc = min(ts, NORM_ROWS)
    nr = ts // rc

    def step(it, carry):
        rows = pl.ds(pl.multiple_of((it % nr) * rc, rc), rc)
        flat = pl.ds(pl.multiple_of(it * rc, rc), rc)
        body(it // nr, rows, flat)
        return carry

    lax.fori_loop(0, nb * nr, step, 0)


def _ada_kernel(c_ref, w_ref, b_ref, o_ref):
    c = c_ref[...]
    cs = (c * jax.nn.sigmoid(c)).astype(BF16)
    o_ref[0] = jnp.dot(cs, w_ref[0].astype(BF16), preferred_element_type=F32) + b_ref[0]


def _ada(c, ada_w, ada_b):
    nl, d, n = ada_w.shape
    bt = c.shape[0]
    return pl.pallas_call(
        _ada_kernel,
        grid=(nl, n // NCHUNK),
        in_specs=[
            pl.BlockSpec((bt, d), lambda l, j: (0, 0)),
            pl.BlockSpec((1, d, NCHUNK), lambda l, j: (l, 0, j)),
            pl.BlockSpec((1, 1, NCHUNK), lambda l, j: (l, 0, j)),
        ],
        out_specs=pl.BlockSpec((1, bt, NCHUNK), lambda l, j: (l, 0, j)),
        out_shape=jax.ShapeDtypeStruct((nl, bt, n), F32),
        compiler_params=_cparams(2),
        name="ada_mod",
    )(c, ada_w, ada_b.reshape(nl, 1, n))


def _norm_kernel(x_ref, *refs, mods):
    norm_refs, out_refs = _take_norm_refs(refs, mods)
    nb, ts, _ = x_ref.shape

    def body(b, rows, flat):
        _emit_norms(x_ref[b, rows, :], b, rows, norm_refs, out_refs)

    _row_loop(nb, ts, body)


def _norm_rows(x, specs):
    b, s, d = x.shape
    nb, ts, spt = _row_tiles(s)
    n_args, n_specs = _norm_inputs(specs, nb, spt)
    out_specs, out_shape = _norm_out(specs, b, s, d, nb, ts, spt)
    return pl.pallas_call(
        functools.partial(_norm_kernel, mods=tuple(sp[1] is not None for sp in specs)),
        grid=(b * s // BM,),
        in_specs=[pl.BlockSpec((nb, ts, d), lambda i: (i // spt, i % spt, 0))] + n_specs,
        out_specs=out_specs,
        out_shape=out_shape,
        compiler_params=_cparams(1),
        name="norm_rows",
    )(x, *n_args)


def _mm_kernel(a_ref, *refs, gelu):
    n_w = len(gelu)
    nb, ts, k = a_ref.shape
    a = a_ref[...].reshape(nb * ts, k)
    for w_ref, o_ref, use_gelu in zip(refs[:n_w], refs[n_w:], gelu):
        for c in range(0, w_ref.shape[1], NCHUNK):
            acc = jnp.dot(a, w_ref[:, c:c + NCHUNK], preferred_element_type=F32)
            if use_gelu:
                acc = jax.nn.gelu(acc)
            o_ref[:, :, c:c + NCHUNK] = acc.reshape(nb, ts, NCHUNK).astype(o_ref.dtype)


def _mm(a, ws, layer, *, gelu, out_dtypes):
    b, s, k = a.shape
    nb, ts, spt = _row_tiles(s)
    row = lambda i: (i // spt, i % spt, 0)
    return pl.pallas_call(
        functools.partial(_mm_kernel, gelu=tuple(gelu)),
        grid=(b * s // BM,),
        in_specs=[pl.BlockSpec((nb, ts, k), row)] + [_resident(w, layer) for w in ws],
        out_specs=[pl.BlockSpec((nb, ts, w.shape[2]), row) for w in ws],
        out_shape=[jax.ShapeDtypeStruct((b, s, w.shape[2]), dt) for w, dt in zip(ws, out_dtypes)],
        compiler_params=_cparams(1),
        name="proj",
    )(a, *ws)


def _kv_kernel(a_ref, wk_ref, wv_ref, kb_ref, vb_ref, kt_ref, vt_ref, *, spt):
    nb, ts, kdim = a_ref.shape
    dh = kt_ref.shape[3]

    def run(with_tail):
        a = a_ref[...].reshape(nb * ts, kdim)
        for w_ref, ob_ref, ot_ref in ((wk_ref, kb_ref, kt_ref), (wv_ref, vb_ref, vt_ref)):
            for c in range(0, w_ref.shape[1], NCHUNK):
                acc = jnp.dot(a, w_ref[:, c:c + NCHUNK], preferred_element_type=F32)
                acc = acc.reshape(nb, ts, NCHUNK)
                ob_ref[:, :, c:c + NCHUNK] = acc.astype(ob_ref.dtype)
                if with_tail:
                    for h in range(NCHUNK // dh):
                        ot_ref[:, :, c // dh + h, :] = acc[:, :, h * dh:(h + 1) * dh]

    if spt == 1:
        run(True)
    else:
        is_tail = pl.program_id(0) % spt == spt - 1
        pl.when(is_tail)(functools.partial(run, True))
        pl.when(jnp.logical_not(is_tail))(functools.partial(run, False))


def _kv_proj(a, wk, wv):
    b, s, k = a.shape
    n = wk.shape[1]
    dh = n // N_HEADS
    nb, ts, spt = _row_tiles(s)
    assert ts == min(LEFT, s), "the returned K/V rows are exactly the last row tile of a sequence"
    row = lambda i: (i // spt, i % spt, 0)
    full = pl.BlockSpec((nb, ts, n), row)
    tail = pl.BlockSpec((nb, ts, N_HEADS, dh), lambda i: (i // spt, 0, 0, 0))
    return pl.pallas_call(
        functools.partial(_kv_kernel, spt=spt),
        grid=(b * s // BM,),
        in_specs=[pl.BlockSpec((nb, ts, k), row), _resident(wk), _resident(wv)],
        out_specs=[full, full, tail, tail],
        out_shape=[jax.ShapeDtypeStruct((b, s, n), BF16)] * 2
        + [jax.ShapeDtypeStruct((b, ts, N_HEADS, dh), F32)] * 2,
        compiler_params=_cparams(1),
        name="kv_proj",
    )(a, wk, wv)


def _proj_res_kernel(a_ref, w_ref, x_ref, gate_ref, *refs, mods):
    norm_refs, (o_ref, *hn_refs) = _take_norm_refs(refs, mods)
    nb, ts, k = a_ref.shape
    a = a_ref[...].reshape(nb * ts, k)
    for c in range(0, w_ref.shape[1], NCHUNK):
        cols = slice(c, c + NCHUNK)
        acc = jnp.dot(a, w_ref[:, cols], preferred_element_type=F32)
        o_ref[:, :, cols] = x_ref[:, :, cols] + gate_ref[:, :, cols] * acc.reshape(nb, ts, NCHUNK)
    rc = min(ts, NORM_ROWS)
    for b in range(nb):
        for r in range(0, ts, rc):
            _emit_norms(o_ref[b, r:r + rc, :], b, slice(r, r + rc), norm_refs, hn_refs)


def _proj_res(a, w, layer, x, gate, specs):
    b, s, d = x.shape
    k = a.shape[2]
    nb, ts, spt = _row_tiles(s)
    row = lambda i: (i // spt, i % spt, 0)
    n_args, n_specs = _norm_inputs(specs, nb, spt)
    out_specs, out_shape = _norm_out(specs, b, s, d, nb, ts, spt)
    return pl.pallas_call(
        functools.partial(_proj_res_kernel, mods=tuple(sp[1] is not None for sp in specs)),
        grid=(b * s // BM,),
        in_specs=[pl.BlockSpec((nb, ts, k), row), _resident(w, layer),
                  pl.BlockSpec((nb, ts, d), row),
                  pl.BlockSpec((nb, 1, d), lambda i: (i // spt, 0, 0))] + n_specs,
        out_specs=[pl.BlockSpec((nb, ts, d), row)] + out_specs,
        out_shape=[jax.ShapeDtypeStruct((b, s, d), F32)] + out_shape,
        compiler_params=_cparams(1),
        name="proj_res",
    )(a, w, x, gate, *n_args)


def _ffn_kernel(hf_ref, x_ref, gate_ref, w1_hbm, w3_hbm, w2_hbm, *refs, mods, emit_x, layer):
    norm_refs, (*out_refs, acc_ref, w1_buf, w3_buf, w2_buf, sem) = _take_norm_refs(refs, mods)
    i = pl.program_id(0)
    nb, ts, d = x_ref.shape
    bf = w1_buf.shape[2]
    nj = w1_hbm.shape[2] // bf
    n_steps = pl.num_programs(0) * nj

    def weight_copies(blk, slot):
        cols = pl.ds(pl.multiple_of(blk * bf, bf), bf)
        return (pltpu.make_async_copy(w1_hbm.at[layer, :, cols], w1_buf.at[slot], sem.at[0, slot]),
                pltpu.make_async_copy(w3_hbm.at[layer, :, cols], w3_buf.at[slot], sem.at[1, slot]),
                pltpu.make_async_copy(w2_hbm.at[layer, cols, :], w2_buf.at[slot], sem.at[2, slot]))

    @pl.when(i == 0)
    def _():
        acc_ref[...] = jnp.zeros(acc_ref.shape, F32)
        for copy in weight_copies(0, 0):
            copy.start()

    def step(j, carry):
        n = i * nj + j
        slot = n % 2

        @pl.when(n + 1 < n_steps)
        def _():
            for copy in weight_copies((j + 1) % nj, 1 - slot):
                copy.start()

        for copy in weight_copies(j, slot):
            copy.wait()
        hf = hf_ref[...].reshape(nb * ts, d)
        h1 = jnp.dot(hf, w1_buf[slot], preferred_element_type=F32)
        h3 = jnp.dot(hf, w3_buf[slot], preferred_element_type=F32)
        act = (h1 * jax.nn.sigmoid(h1) * h3).astype(BF16)
        acc_ref[...] += jnp.dot(act, w2_buf[slot], preferred_element_type=F32)
        return carry

    lax.fori_loop(0, nj, step, 0)

    def body(b, rows, flat):
        ff = acc_ref[flat, :]
        acc_ref[flat, :] = jnp.zeros_like(ff)
        y = x_ref[b, rows, :] + gate_ref[b] * ff
        if emit_x:
            out_refs[0][b, rows, :] = y
        _emit_norms(y, b, rows, norm_refs, out_refs[emit_x:])

    _row_loop(nb, ts, body)


def _ffn(hf, x, gate, w1, w3, w2, layer, specs, *, emit_x):
    b, s, d = x.shape
    nb, ts, spt = _row_tiles(s)
    row = lambda i: (i // spt, i % spt, 0)
    n_args, n_specs = _norm_inputs(specs, nb, spt)
    out_specs, out_shape = _norm_out(specs, b, s, d, nb, ts, spt)
    if emit_x:
        out_specs = [pl.BlockSpec((nb, ts, d), row)] + out_specs
        out_shape = [jax.ShapeDtypeStruct((b, s, d), F32)] + out_shape
    hbm = pl.BlockSpec(memory_space=pl.ANY)
    return pl.pallas_call(
        functools.partial(_ffn_kernel, mods=tuple(sp[1] is not None for sp in specs),
                          emit_x=emit_x, layer=layer),
        grid=(b * s // BM,),
        in_specs=[
            pl.BlockSpec((nb, ts, d), row), pl.BlockSpec((nb, ts, d), row),
            pl.BlockSpec((nb, 1, d), lambda i: (i // spt, 0, 0)),
            hbm, hbm, hbm,
        ] + n_specs,
        out_specs=out_specs,
        out_shape=out_shape,
        scratch_shapes=[pltpu.VMEM((BM, d), F32),
                        pltpu.VMEM((2, d, FFN_COLS), BF16), pltpu.VMEM((2, d, FFN_COLS), BF16),
                        pltpu.VMEM((2, FFN_COLS, d), BF16), pltpu.SemaphoreType.DMA((3, 2))],
        compiler_params=_cparams(1),
        name="ffn",
    )(hf, x, gate, w1, w3, w2, *n_args)


def _scan8(a, b):
    row = lax.broadcasted_iota(jnp.int32, a.shape, 0)
    for k in (1, 2, 4):
        a_sh = jnp.where(row >= k, pltpu.roll(a, k, axis=0), 1.0)
        b_sh = jnp.where(row >= k, pltpu.roll(b, k, axis=0), 0.0)
        b = a * b_sh + b
        a = a * a_sh
    return a, b


def _rglru_kernel(xb_ref, gate_ref, cprev_ref, h0_ref, cw_ref, cb_ref, wa_ref, wi_ref,
                  ba_ref, bi_ref, lam_ref, hg_ref, cnew_ref, hlast_ref,
                  xpad_ref, a_ref, u_ref, hc_ref, *, pos0):
    t = pl.program_id(1)
    ts, d = xb_ref.shape[1:]
    bw = d // RG_BLOCKS
    hist = CONV_W - 1
    h_lo = CONV_PAD - hist

    @pl.when(t == 0)
    def _():
        xpad_ref[h_lo:CONV_PAD, :] = cprev_ref[0]
        hc_ref[...] = jnp.broadcast_to(h0_ref[0], hc_ref.shape)

    @pl.when(t > 0)
    def _():
        xpad_ref[h_lo:CONV_PAD, :] = xpad_ref[ts + h_lo:ts + CONV_PAD, :]

    xpad_ref[CONV_PAD:CONV_PAD + ts, :] = xb_ref[0]

    first = (pos0 + t * ts + lax.broadcasted_iota(jnp.int32, (ts, bw), 0)) == 0
    for n in range(RG_BLOCKS):
        cs = slice(n * bw, (n + 1) * bw)
        xp = xpad_ref[:, cs]
        xc = cb_ref[:, cs] + xp * cw_ref[0:1, cs]
        for k in range(1, CONV_W):
            xc = pltpu.roll(xc, 1, axis=0) + xp * cw_ref[k:k + 1, cs]
        xc = xc[CONV_PAD:, :]
        xcb = xc.astype(BF16)
        tr = jnp.tanh(jnp.dot(xcb, wa_ref[n], preferred_element_type=F32) + 0.5 * ba_ref[:, cs])
        ti = jnp.tanh(jnp.dot(xcb, wi_ref[n], preferred_element_type=F32) + 0.5 * bi_ref[:, cs])
        lam = lam_ref[:, cs]
        softplus = jnp.maximum(-lam, 0.0) + jnp.log1p(jnp.exp(-jnp.abs(lam)))
        log_a = (tr + 1.0) * (-0.5 * RG_C * softplus)
        a = jnp.exp(log_a)
        z = -jnp.tanh(log_a) * (a * a + 1.0)
        mult = jnp.where(z == 0.0, 0.0, z * lax.rsqrt(z))
        if pos0 <= 0:
            mult = jnp.where(first, 1.0, mult)
        a_ref[:, cs] = a
        u_ref[:, cs] = mult * ((0.5 * ti + 0.5) * xc)

    for c in range(0, d, SCAN_LANES):
        cs = slice(c, c + SCAN_LANES)

        def group(gi, hprev, cs=cs):
            rows = pl.ds(pl.multiple_of(gi * SUBLANES, SUBLANES), SUBLANES)
            a_cum, h_loc = _scan8(a_ref[rows, cs], u_ref[rows, cs])
            h = h_loc + a_cum * hprev
            u_ref[rows, cs] = h
            return jnp.broadcast_to(h[SUBLANES - 1:SUBLANES, :], h.shape)

        hc_ref[:, cs] = lax.fori_loop(0, ts // SUBLANES, group, hc_ref[:, cs], unroll=2)

    hg_ref[0] = (u_ref[...] * gate_ref[0]).astype(hg_ref.dtype)

    @pl.when(t == pl.num_programs(1) - 1)
    def _():
        cnew_ref[0] = xpad_ref[ts + h_lo:ts + CONV_PAD, :]
        hlast_ref[0] = hc_ref[0:1, :]


def _rglru(gate, xb, conv_prev, h0, cw, cb, wa, wi, layer, ba, bi, lam, *, pos0):
    b, s, d = xb.shape
    ts = min(s, RG_ROWS)
    hist = CONV_W - 1
    vec = lambda a: a.reshape(1, d)
    tile = pl.BlockSpec((1, ts, d), lambda i, t: (i, t, 0))
    vec_spec = pl.BlockSpec((1, d), lambda i, t: (0, 0))
    hg, cnew, hlast = pl.pallas_call(
        functools.partial(_rglru_kernel, pos0=pos0),
        grid=(b, s // ts),
        in_specs=[
            tile, tile,
            pl.BlockSpec((1, hist, d), lambda i, t: (i, 0, 0)),
            pl.BlockSpec((1, 1, d), lambda i, t: (i, 0, 0)),
            pl.BlockSpec((CONV_W, d), lambda i, t: (0, 0)),
            vec_spec, _resident(wa, layer), _resident(wi, layer), vec_spec, vec_spec, vec_spec,
        ],
        out_specs=[
            tile,
            pl.BlockSpec((1, hist, d), lambda i, t: (i, 0, 0)),
            pl.BlockSpec((1, 1, d), lambda i, t: (i, 0, 0)),
        ],
        out_shape=[
            jax.ShapeDtypeStruct((b, s, d), BF16),
            jax.ShapeDtypeStruct((b, hist, d), F32),
            jax.ShapeDtypeStruct((b, 1, d), F32),
        ],
        scratch_shapes=[
            pltpu.VMEM((CONV_PAD + ts, d), F32),
            pltpu.VMEM((ts, d), F32),
            pltpu.VMEM((ts, d), F32),
            pltpu.VMEM((SUBLANES, d), F32),
        ],
        compiler_params=_cparams(2),
        name="rglru",
    )(xb, gate, conv_prev, h0, cw, vec(cb), wa, wi, vec(ba), vec(bi), vec(lam))
    return hg, cnew, hlast.reshape(b, d)


def _build_bias(bt_ref, bias_ref):
    hb, tq, w = bias_ref.shape
    wb = bt_ref.shape[2]
    qi = lax.broadcasted_iota(jnp.int32, (tq, w), 0)
    kj = lax.broadcasted_iota(jnp.int32, (tq, w), 1)
    lo = (qi // CHUNK) * CHUNK
    band = (kj >= lo) & (kj < lo + BAND)
    for hh in range(hb):
        bias = pltpu.roll(jnp.broadcast_to(bt_ref[hh], (tq, wb)), 0, axis=1,
                          stride=1, stride_axis=0)[:, :w]
        bias_ref[hh] = jnp.where(band, bias * LOG2E, NEG)


def _softmax_pv(q, k, v, bias, scale):
    sc = lax.dot_general(q, k, (((1,), (1,)), ((), ())), preferred_element_type=F32)
    sc = sc * (scale * LOG2E) + bias
    m = jnp.max(sc, axis=-1, keepdims=True)
    p = jnp.exp2(sc - m)
    l = jnp.sum(p, axis=-1, keepdims=True)
    return jnp.dot(p.astype(BF16), v, preferred_element_type=F32) / l


def _attn_direct_kernel(q_ref, k_ref, v_ref, bt_ref, o_ref, bias_ref, *, scale):
    b = pl.program_id(1)
    t = pl.program_id(2)
    hb, tq, w = bias_ref.shape
    dh = q_ref.shape[2] // hb

    @pl.when((b == 0) & (t == 0))
    def _():
        _build_bias(bt_ref, bias_ref)

    def tile(row0, wp):
        for hh in range(hb):
            cs = slice(hh * dh, (hh + 1) * dh)
            o = _softmax_pv(q_ref[0, :, cs], k_ref[0, pl.ds(row0, wp), cs],
                            v_ref[0, pl.ds(row0, wp), cs], bias_ref[hh, :, w - wp:], scale)
            o_ref[0, :, cs] = o.astype(o_ref.dtype)

    n_early = LEFT // tq
    for e in range(n_early):
        pl.when(t == e)(functools.partial(tile, 0, (e + 1) * tq))

    @pl.when(t >= n_early)
    def _():
        tile(pl.multiple_of(t * tq - LEFT, tq), w)


def _attn_staged_kernel(q_ref, k_ref, v_ref, kc_ref, vc_ref, bt_ref, o_ref,
                        bias_ref, kp_ref, vp_ref, *, scale):
    b = pl.program_id(1)
    t = pl.program_id(2)
    hb, tq, w = bias_ref.shape
    s = k_ref.shape[1]
    rows, dh = kp_ref.shape[1:]

    @pl.when((b == 0) & (t == 0))
    def _():
        _build_bias(bt_ref, bias_ref)

    @pl.when(t == 0)
    def _():
        for hh in range(hb):
            cs = slice(hh * dh, (hh + 1) * dh)
            kp_ref[hh, 0:LEFT, :] = kc_ref[0, :, hh, :].astype(BF16)
            vp_ref[hh, 0:LEFT, :] = vc_ref[0, :, hh, :].astype(BF16)
            kp_ref[hh, LEFT:LEFT + s, :] = k_ref[0, :, cs]
            vp_ref[hh, LEFT:LEFT + s, :] = v_ref[0, :, cs]
            if rows > LEFT + s:
                kp_ref[hh, LEFT + s:, :] = jnp.zeros((rows - LEFT - s, dh), BF16)
                vp_ref[hh, LEFT + s:, :] = jnp.zeros((rows - LEFT - s, dh), BF16)

    q0 = pl.multiple_of(t * tq, tq)
    for hh in range(hb):
        cs = slice(hh * dh, (hh + 1) * dh)
        o = _softmax_pv(q_ref[0, :, cs], kp_ref[hh, pl.ds(q0, w), :], vp_ref[hh, pl.ds(q0, w), :],
                        bias_ref[hh], scale)
        o_ref[0, :, cs] = o.astype(o_ref.dtype)


def _bias_row(rel_table, wb):
    far = rel_table[2 * REL_CLIP]
    n_left = LEFT - REL_CLIP
    n_right = wb - n_left - (2 * REL_CLIP + 1)
    row = jnp.concatenate([
        jnp.broadcast_to(far, (n_left,) + far.shape), rel_table[::-1],
        jnp.broadcast_to(far, (n_right,) + far.shape)], axis=0)
    return row.T.reshape(rel_table.shape[1], 1, wb)


def _attention(q, k, v, k_cache, v_cache, rel_table):
    b, s, d = q.shape
    dh = d // N_HEADS
    tq = min(s, ATTN_ROWS)
    hb = ATTN_HEADS if s > tq else N_HEADS
    w = -(-(tq + LEFT) // LANES) * LANES
    wb = max(w, LEFT + REL_CLIP + 2 * LANES)
    qspec = pl.BlockSpec((1, tq, dh * hb), lambda h, i, t: (i, t, h))
    kspec = pl.BlockSpec((1, s, dh * hb), lambda h, i, t: (i, 0, h))
    bspec = pl.BlockSpec((hb, 1, wb), lambda h, i, t: (h, 0, 0))
    common = dict(
        grid=(N_HEADS // hb, b, s // tq),
        out_specs=qspec,
        out_shape=jax.ShapeDtypeStruct((b, s, d), BF16),
        compiler_params=_cparams(3),
    )
    bias_scratch = pltpu.VMEM((hb, tq, w), F32)
    bias_rows = _bias_row(rel_table, wb)
    if k_cache is None:
        assert LEFT % tq == 0 and tq % LANES == 0 and s % tq == 0
        return pl.pallas_call(
            functools.partial(_attn_direct_kernel, scale=dh ** -0.5),
            in_specs=[qspec, kspec, kspec, bspec],
            scratch_shapes=[bias_scratch],
            name="band_attn", **common,
        )(q, k, v, bias_rows)
    assert k_cache.shape[1] == LEFT, "the past band buffer holds exactly LEFT rows"
    rows = (s - tq) + w
    cspec = pl.BlockSpec((1, LEFT, hb, dh), lambda h, i, t: (i, 0, h, 0))
    return pl.pallas_call(
        functools.partial(_attn_staged_kernel, scale=dh ** -0.5),
        in_specs=[qspec, kspec, kspec, cspec, cspec, bspec],
        scratch_shapes=[bias_scratch, pltpu.VMEM((hb, rows, dh), BF16),
                        pltpu.VMEM((hb, rows, dh), BF16)],
        name="band_attn_past", **common,
    )(q, k, v, k_cache, v_cache, bias_rows)


def _trunk(x, mod, pos0, conv_state, rnn_state, k_cache, v_cache, p):
    b, s, d = x.shape
    depth = mod.shape[0]
    n_a = p['rg_w_in'].shape[0]
    mods = [[m.reshape(b, 1, d) for m in jnp.split(mod[layer], 6, axis=-1)] for layer in range(depth)]
    mix_spec = lambda l: (p['g_mix'][l], mods[l][0], mods[l][1], BF16)
    kv_spec = (p['g_kv'], None, None, BF16)

    def next_specs(l):
        if l == depth:
            return [(p['g_final'], None, None, F32)]
        return [mix_spec(l)] + ([kv_spec] if l == n_a else [])

    normed = _norm_rows(x, next_specs(0))
    conv_out, rnn_out = [], []
    k_new = v_new = k_tail = v_tail = None
    for layer in range(depth):
        _, _, g1, sh2, sc2, g2 = mods[layer]
        if layer == n_a:
            k_new, v_new, k_tail, v_tail = _kv_proj(normed[1], p['w_k'], p['w_v'])
        if layer < n_a:
            gate, xb = _mm(normed[0], [p['rg_w_gate'], p['rg_w_in']], layer,
                           gelu=(True, False), out_dtypes=(F32, F32))
            cprev = jnp.zeros((b, CONV_W - 1, d), F32) if conv_state is None else conv_state[layer]
            h0 = jnp.zeros((b, 1, d), F32) if rnn_state is None else rnn_state[layer].reshape(b, 1, d)
            mix, cst, hst = _rglru(gate, xb, cprev, h0, p['rg_conv_w'][layer], p['rg_conv_b'][layer],
                                   p['rg_w_a'], p['rg_w_i'], layer, p['rg_b_a'][layer],
                                   p['rg_b_i'][layer], p['rg_lambda'][layer], pos0=pos0)
            conv_out.append(cst)
            rnn_out.append(hst)
            w_out, w_layer = p['rg_w_out'], layer
        else:
            bl = layer - n_a
            q, = _mm(normed[0], [p['w_q']], bl, gelu=(False,), out_dtypes=(BF16,))
            mix = _attention(q, k_new, v_new, k_cache, v_cache, p['rel_bias'][bl])
            w_out, w_layer = p['w_o'], bl
        x, hf = _proj_res(mix, w_out, w_layer, x, g1, [(p['g_ffn'][layer], sh2, sc2, BF16)])
        last = layer == depth - 1
        outs = _ffn(hf, x, g2, p['ffn_w1'], p['ffn_w3'], p['ffn_w2'], layer,
                    next_specs(layer + 1), emit_x=not last)
        if last:
            x = outs[0]
        else:
            x, normed = outs[0], outs[1:]
    return x, jnp.stack(conv_out), jnp.stack(rnn_out), k_tail, v_tail


def kernel(x_prompt, x_sample, c_prompt, c_sample, state_conv, state_rnn, cache_k, cache_v, ada_w, ada_b, g_mix, g_ffn, rg_w_in, rg_w_gate, rg_conv_w, rg_conv_b, rg_w_a, rg_b_a, rg_w_i, rg_b_i, rg_lambda, rg_w_out, g_kv, w_k, w_v, w_q, w_o, rel_bias, ffn_w1, ffn_w3, ffn_w2, g_final):
    p = {
        'g_mix': g_mix, 'g_ffn': g_ffn, 'g_kv': g_kv, 'g_final': g_final,
        'rg_w_gate': rg_w_gate.astype(BF16), 'rg_w_in': rg_w_in.astype(BF16),
        'rg_conv_w': rg_conv_w, 'rg_conv_b': rg_conv_b,
        'rg_w_a': (0.5 * rg_w_a).astype(BF16), 'rg_b_a': rg_b_a,
        'rg_w_i': (0.5 * rg_w_i).astype(BF16), 'rg_b_i': rg_b_i,
        'rg_lambda': rg_lambda, 'rg_w_out': rg_w_out.astype(BF16),
        'w_k': w_k.astype(BF16), 'w_v': w_v.astype(BF16),
        'w_q': w_q.astype(BF16), 'w_o': w_o.astype(BF16), 'rel_bias': rel_bias,
        'ffn_w1': ffn_w1.astype(BF16), 'ffn_w3': ffn_w3.astype(BF16), 'ffn_w2': ffn_w2.astype(BF16),
    }
    nbp = c_prompt.shape[0]
    mod = _ada(jnp.concatenate([c_prompt, c_sample], axis=0), ada_w, ada_b)

    y_p, conv_p, rnn_p, k_p, v_p = _trunk(x_prompt, mod[:, :nbp], 0, None, None, None, None, p)
    y_s, conv_s, rnn_s, k_s, v_s = _trunk(
        x_sample, mod[:, nbp:], PAST_LEN, state_conv, state_rnn, cache_k, cache_v, p)
    return (y_p, y_s, conv_p, rnn_p, k_p, v_p, conv_s, rnn_s, k_s, v_s)
```

```python
import functools

import jax
import jax.numpy as jnp
from jax import lax
from jax.experimental import pallas as pl
from jax.experimental.pallas import tpu as pltpu

F32 = jnp.float32
BF16 = jnp.bfloat16

CHUNK = 64
LEFT = 512
BAND = LEFT + CHUNK
REL_CLIP = 128
RG_C = 8.0
RG_BLOCKS = 8
CONV_W = 4
N_HEADS = 16
EPS = 1e-6
PAST_LEN = 2048
NEG = -1e30
LOG2E = 1.4426950408889634

LANES = 128
SUBLANES = 8
VMEM_LIMIT = 56 * 1024 * 1024

BM = 512
NCHUNK = 1024
NORM_ROWS = 128
CONV_PAD = 8
FFN_ROWS = 1024
FFN_COLS = 512
FFN_REFILL_STEP = 2
SCAN_LANES = 1024
RG_ROWS = 256
ATTN_ROWS = 256
ATTN_HEADS = 8


def _cparams(n_axes):
    return pltpu.CompilerParams(
        dimension_semantics=("arbitrary",) * n_axes, vmem_limit_bytes=VMEM_LIMIT)


def _row_tiles(s, bm=BM):
    ts = min(s, bm)
    return bm // ts, ts, s // ts


def _resident(w, layer=None):
    if layer is None:
        shape, index = w.shape, (0,) * w.ndim
    else:
        shape, index = (None,) + w.shape[1:], (layer,) + (0,) * (w.ndim - 1)
    return pl.BlockSpec(shape, lambda *idx: index, pipeline_mode=pl.Buffered(1))


def _norm_inputs(specs, nb, spt):
    args, in_specs = [], []
    for g, sh, sc, _ in specs:
        d = g.shape[-1]
        args.append(g.reshape(1, d))
        in_specs.append(pl.BlockSpec((1, d), lambda *idx: (0, 0)))
        if sh is not None:
            mod_spec = pl.BlockSpec((nb, 1, d), lambda *idx: (idx[0] // spt, 0, 0))
            args += [sh, sc]
            in_specs += [mod_spec, mod_spec]
    return args, in_specs


def _take_norm_refs(refs, mods):
    out, pos = [], 0
    for has_mod in mods:
        if has_mod:
            out.append(refs[pos:pos + 3])
            pos += 3
        else:
            out.append((refs[pos], None, None))
            pos += 1
    return out, refs[pos:]


def _emit_norms(y, b, rows, norm_refs, out_refs):
    yn = y * lax.rsqrt(jnp.mean(y * y, axis=-1, keepdims=True) + EPS)
    for (g_ref, sh_ref, sc_ref), o_ref in zip(norm_refs, out_refs):
        v = yn * g_ref[...]
        if sh_ref is not None:
            v = v * (1.0 + sc_ref[b]) + sh_ref[b]
        o_ref[b, rows, :] = v.astype(o_ref.dtype)


def _norm_out(specs, b, s, d, nb, ts, spt):
    out_specs = [pl.BlockSpec((nb, ts, d), lambda *idx: (idx[0] // spt, idx[0] % spt, 0))
                 for _ in specs]
    out_shape = [jax.ShapeDtypeStruct((b, s, d), spec[3]) for spec in specs]
    return out_specs, out_shape


def _row_loop(nb, ts, body):
    rc = min(ts, NORM_ROWS)
    nr = ts // rc

    def step(it, carry):
        rows = pl.ds(pl.multiple_of((it % nr) * rc, rc), rc)
        flat = pl.ds(pl.multiple_of(it * rc, rc), rc)
        body(it // nr, rows, flat)
        return carry

    lax.fori_loop(0, nb * nr, step, 0)


def _ada_kernel(c_ref, w_ref, b_ref, o_ref):
    c = c_ref[...]
    cs = (c * jax.nn.sigmoid(c)).astype(BF16)
    o_ref[0] = jnp.dot(cs, w_ref[0].astype(BF16), preferred_element_type=F32) + b_ref[0]


def _ada(c, ada_w, ada_b):
    nl, d, n = ada_w.shape
    bt = c.shape[0]
    return pl.pallas_call(
        _ada_kernel,
        grid=(nl, n // NCHUNK),
        in_specs=[
            pl.BlockSpec((bt, d), lambda l, j: (0, 0)),
            pl.BlockSpec((1, d, NCHUNK), lambda l, j: (l, 0, j)),
            pl.BlockSpec((1, 1, NCHUNK), lambda l, j: (l, 0, j)),
        ],
        out_specs=pl.BlockSpec((1, bt, NCHUNK), lambda l, j: (l, 0, j)),
        out_shape=jax.ShapeDtypeStruct((nl, bt, n), F32),
        compiler_params=_cparams(2),
        name="ada_mod",
    )(c, ada_w, ada_b.reshape(nl, 1, n))


def _norm_kernel(x_ref, *refs, mods):
    norm_refs, out_refs = _take_norm_refs(refs, mods)
    nb, ts, _ = x_ref.shape

    def body(b, rows, flat):
        _emit_norms(x_ref[b, rows, :], b, rows, norm_refs, out_refs)

    _row_loop(nb, ts, body)


def _norm_rows(x, specs):
    b, s, d = x.shape
    nb, ts, spt = _row_tiles(s)
    n_args, n_specs = _norm_inputs(specs, nb, spt)
    out_specs, out_shape = _norm_out(specs, b, s, d, nb, ts, spt)
    return pl.pallas_call(
        functools.partial(_norm_kernel, mods=tuple(sp[1] is not None for sp in specs)),
        grid=(b * s // BM,),
        in_specs=[pl.BlockSpec((nb, ts, d), lambda i: (i // spt, i % spt, 0))] + n_specs,
        out_specs=out_specs,
        out_shape=out_shape,
        compiler_params=_cparams(1),
        name="norm_rows",
    )(x, *n_args)


def _mm_kernel(a_ref, *refs, gelu):
    n_w = len(gelu)
    nb, ts, k = a_ref.shape
    a = a_ref[...].reshape(nb * ts, k)
    for w_ref, o_ref, use_gelu in zip(refs[:n_w], refs[n_w:], gelu):
        for c in range(0, w_ref.shape[1], NCHUNK):
            acc = jnp.dot(a, w_ref[:, c:c + NCHUNK], preferred_element_type=F32)
            if use_gelu:
                acc = jax.nn.gelu(acc)
            o_ref[:, :, c:c + NCHUNK] = acc.reshape(nb, ts, NCHUNK).astype(o_ref.dtype)


def _mm(a, ws, layer, *, gelu, out_dtypes):
    b, s, k = a.shape
    nb, ts, spt = _row_tiles(s)
    row = lambda i: (i // spt, i % spt, 0)
    return pl.pallas_call(
        functools.partial(_mm_kernel, gelu=tuple(gelu)),
        grid=(b * s // BM,),
        in_specs=[pl.BlockSpec((nb, ts, k), row)] + [_resident(w, layer) for w in ws],
        out_specs=[pl.BlockSpec((nb, ts, w.shape[2]), row) for w in ws],
        out_shape=[jax.ShapeDtypeStruct((b, s, w.shape[2]), dt) for w, dt in zip(ws, out_dtypes)],
        compiler_params=_cparams(1),
        name="proj",
    )(a, *ws)


def _kv_kernel(a_ref, wk_ref, wv_ref, kb_ref, vb_ref, kt_ref, vt_ref, *, spt):
    nb, ts, kdim = a_ref.shape
    dh = kt_ref.shape[3]

    def run(with_tail):
        a = a_ref[...].reshape(nb * ts, kdim)
        for w_ref, ob_ref, ot_ref in ((wk_ref, kb_ref, kt_ref), (wv_ref, vb_ref, vt_ref)):
            for c in range(0, w_ref.shape[1], NCHUNK):
                acc = jnp.dot(a, w_ref[:, c:c + NCHUNK], preferred_element_type=F32)
                acc = acc.reshape(nb, ts, NCHUNK)
                ob_ref[:, :, c:c + NCHUNK] = acc.astype(ob_ref.dtype)
                if with_tail:
                    for h in range(NCHUNK // dh):
                        ot_ref[:, :, c // dh + h, :] = acc[:, :, h * dh:(h + 1) * dh]

    if spt == 1:
        run(True)
    else:
        is_tail = pl.program_id(0) % spt == spt - 1
        pl.when(is_tail)(functools.partial(run, True))
        pl.when(jnp.logical_not(is_tail))(functools.partial(run, False))


def _kv_proj(a, wk, wv):
    b, s, k = a.shape
    n = wk.shape[1]
    dh = n // N_HEADS
    nb, ts, spt = _row_tiles(s)
    assert ts == min(LEFT, s), "the returned K/V rows are exactly the last row tile of a sequence"
    row = lambda i: (i // spt, i % spt, 0)
    full = pl.BlockSpec((nb, ts, n), row)
    tail = pl.BlockSpec((nb, ts, N_HEADS, dh), lambda i: (i // spt, 0, 0, 0))
    return pl.pallas_call(
        functools.partial(_kv_kernel, spt=spt),
        grid=(b * s // BM,),
        in_specs=[pl.BlockSpec((nb, ts, k), row), _resident(wk), _resident(wv)],
        out_specs=[full, full, tail, tail],
        out_shape=[jax.ShapeDtypeStruct((b, s, n), BF16)] * 2
        + [jax.ShapeDtypeStruct((b, ts, N_HEADS, dh), F32)] * 2,
        compiler_params=_cparams(1),
        name="kv_proj",
    )(a, wk, wv)


def _proj_res_kernel(a_ref, w_ref, x_ref, gate_ref, *refs, mods):
    norm_refs, (o_ref, *hn_refs) = _take_norm_refs(refs, mods)
    nb, ts, k = a_ref.shape
    a = a_ref[...].reshape(nb * ts, k)
    for c in range(0, w_ref.shape[1], NCHUNK):
        cols = slice(c, c + NCHUNK)
        acc = jnp.dot(a, w_ref[:, cols], preferred_element_type=F32)
        o_ref[:, :, cols] = x_ref[:, :, cols] + gate_ref[:, :, cols] * acc.reshape(nb, ts, NCHUNK)
    rc = min(ts, NORM_ROWS)
    for b in range(nb):
        for r in range(0, ts, rc):
            _emit_norms(o_ref[b, r:r + rc, :], b, slice(r, r + rc), norm_refs, hn_refs)


def _proj_res(a, w, layer, x, gate, specs):
    b, s, d = x.shape
    k = a.shape[2]
    nb, ts, spt = _row_tiles(s)
    row = lambda i: (i // spt, i % spt, 0)
    n_args, n_specs = _norm_inputs(specs, nb, spt)
    out_specs, out_shape = _norm_out(specs, b, s, d, nb, ts, spt)
    return pl.pallas_call(
        functools.partial(_proj_res_kernel, mods=tuple(sp[1] is not None for sp in specs)),
        grid=(b * s // BM,),
        in_specs=[pl.BlockSpec((nb, ts, k), row), _resident(w, layer),
                  pl.BlockSpec((nb, ts, d), row),
                  pl.BlockSpec((nb, 1, d), lambda i: (i // spt, 0, 0))] + n_specs,
        out_specs=[pl.BlockSpec((nb, ts, d), row)] + out_specs,
        out_shape=[jax.ShapeDtypeStruct((b, s, d), F32)] + out_shape,
        compiler_params=_cparams(1),
        name="proj_res",
    )(a, w, x, gate, *n_args)


def _ffn_kernel(hf_ref, gate_ref, x_hbm, w1_hbm, w3_hbm, w2_hbm, *refs, mods, emit_x, layer, spt):
    norm_refs, rest = _take_norm_refs(refs, mods)
    n_out = emit_x + len(mods)
    out_hbm, (acc_ref, x_buf, *rest) = rest[:n_out], rest[n_out:]
    norm_bufs, (w1_buf, w3_buf, w2_buf, wsem, xsem, osem) = rest[:len(mods)], rest[len(mods):]
    out_bufs = ([x_buf] if emit_x else []) + list(norm_bufs)
    i = pl.program_id(0)
    nt = pl.num_programs(0)
    nb, ts, d = x_buf.shape
    bf = w1_buf.shape[2]
    nj = w1_hbm.shape[2] // bf
    n_steps = nt * nj

    def weight_copies(blk, slot):
        cols = pl.ds(pl.multiple_of(blk * bf, bf), bf)
        return (pltpu.make_async_copy(w1_hbm.at[layer, :, cols], w1_buf.at[slot], wsem.at[0, slot]),
                pltpu.make_async_copy(w3_hbm.at[layer, :, cols], w3_buf.at[slot], wsem.at[1, slot]),
                pltpu.make_async_copy(w2_hbm.at[layer, cols, :], w2_buf.at[slot], wsem.at[2, slot]))

    def tile_of(ref, t):
        rows = pl.ds(0, ts) if spt == 1 else pl.ds(pl.multiple_of((t % spt) * ts, ts), ts)
        return ref.at[pl.ds(pl.multiple_of((t // spt) * nb, nb), nb), rows, :]

    def x_copy(t):
        return pltpu.make_async_copy(tile_of(x_hbm, t), x_buf, xsem.at[0])

    def out_copies(t):
        return [pltpu.make_async_copy(buf, tile_of(hbm, t), osem.at[k])
                for k, (buf, hbm) in enumerate(zip(out_bufs, out_hbm))]

    @pl.when(i == 0)
    def _():
        acc_ref[...] = jnp.zeros(acc_ref.shape, F32)
        for copy in weight_copies(0, 0):
            copy.start()
        x_copy(0).start()

    def step(j, carry):
        n = i * nj + j
        slot = n % 2

        @pl.when(n + 1 < n_steps)
        def _():
            for copy in weight_copies((j + 1) % nj, 1 - slot):
                copy.start()

        @pl.when((j == FFN_REFILL_STEP) & (i > 0))
        def _():
            for copy in out_copies(i - 1):
                copy.wait()
            x_copy(i).start()

        for copy in weight_copies(j, slot):
            copy.wait()
        hf = hf_ref[...].reshape(nb * ts, d)
        h1 = jnp.dot(hf, w1_buf[slot], preferred_element_type=F32)
        h3 = jnp.dot(hf, w3_buf[slot], preferred_element_type=F32)
        act = (h1 * jax.nn.sigmoid(h1) * h3).astype(BF16)
        acc_ref[...] += jnp.dot(act, w2_buf[slot], preferred_element_type=F32)
        return carry

    lax.fori_loop(0, nj, step, 0)
    x_copy(i).wait()

    def body(b, rows, flat):
        ff = acc_ref[flat, :]
        acc_ref[flat, :] = jnp.zeros_like(ff)
        y = x_buf[b, rows, :] + gate_ref[b] * ff
        if emit_x:
            x_buf[b, rows, :] = y
        _emit_norms(y, b, rows, norm_refs, norm_bufs)

    _row_loop(nb, ts, body)
    for copy in out_copies(i):
        copy.start()

    @pl.when(i == nt - 1)
    def _():
        for copy in out_copies(i):
            copy.wait()


def _ffn(hf, x, gate, w1, w3, w2, layer, specs, *, emit_x):
    b, s, d = x.shape
    nb, ts, spt = _row_tiles(s, FFN_ROWS)
    assert w1.shape[2] // FFN_COLS > FFN_REFILL_STEP
    n_args, n_specs = _norm_inputs(specs, nb, spt)
    out_shape = [jax.ShapeDtypeStruct((b, s, d), sp[3]) for sp in specs]
    if emit_x:
        out_shape = [jax.ShapeDtypeStruct((b, s, d), F32)] + out_shape
    hbm = pl.BlockSpec(memory_space=pl.ANY)
    return pl.pallas_call(
        functools.partial(_ffn_kernel, mods=tuple(sp[1] is not None for sp in specs),
                          emit_x=emit_x, layer=layer, spt=spt),
        grid=(b * s // FFN_ROWS,),
        in_specs=[
            pl.BlockSpec((nb, ts, d), lambda i: (i // spt, i % spt, 0)),
            pl.BlockSpec((nb, 1, d), lambda i: (i // spt, 0, 0)),
            hbm, hbm, hbm, hbm,
        ] + n_specs,
        out_specs=[hbm] * len(out_shape),
        out_shape=out_shape,
        scratch_shapes=[pltpu.VMEM((FFN_ROWS, d), F32), pltpu.VMEM((nb, ts, d), F32)]
        + [pltpu.VMEM((nb, ts, d), sp[3]) for sp in specs]
        + [pltpu.VMEM((2, d, FFN_COLS), BF16), pltpu.VMEM((2, d, FFN_COLS), BF16),
           pltpu.VMEM((2, FFN_COLS, d), BF16), pltpu.SemaphoreType.DMA((3, 2)),
           pltpu.SemaphoreType.DMA((1,)), pltpu.SemaphoreType.DMA((len(out_shape),))],
        compiler_params=_cparams(1),
        name="ffn",
    )(hf, gate, x, w1, w3, w2, *n_args)


def _scan8(a, b):
    row = lax.broadcasted_iota(jnp.int32, a.shape, 0)
    for k in (1, 2, 4):
        a_sh = jnp.where(row >= k, pltpu.roll(a, k, axis=0), 1.0)
        b_sh = jnp.where(row >= k, pltpu.roll(b, k, axis=0), 0.0)
        b = a * b_sh + b
        a = a * a_sh
    return a, b


def _rglru_kernel(xb_ref, gate_ref, cprev_ref, h0_ref, cw_ref, cb_ref, wa_ref, wi_ref,
                  ba_ref, bi_ref, lam_ref, hg_ref, cnew_ref, hlast_ref,
                  xpad_ref, a_ref, u_ref, hc_ref, *, pos0):
    t = pl.program_id(1)
    ts, d = xb_ref.shape[1:]
    bw = d // RG_BLOCKS
    hist = CONV_W - 1
    h_lo = CONV_PAD - hist

    @pl.when(t == 0)
    def _():
        xpad_ref[h_lo:CONV_PAD, :] = cprev_ref[0]
        hc_ref[...] = jnp.broadcast_to(h0_ref[0], hc_ref.shape)

    @pl.when(t > 0)
    def _():
        xpad_ref[h_lo:CONV_PAD, :] = xpad_ref[ts + h_lo:ts + CONV_PAD, :]

    xpad_ref[CONV_PAD:CONV_PAD + ts, :] = xb_ref[0]

    first = (pos0 + t * ts + lax.broadcasted_iota(jnp.int32, (ts, bw), 0)) == 0
    for n in range(RG_BLOCKS):
        cs = slice(n * bw, (n + 1) * bw)
        xp = xpad_ref[:, cs]
        xc = cb_ref[:, cs] + xp * cw_ref[0:1, cs]
        for k in range(1, CONV_W):
            xc = pltpu.roll(xc, 1, axis=0) + xp * cw_ref[k:k + 1, cs]
        xc = xc[CONV_PAD:, :]
        xcb = xc.astype(BF16)
        tr = jnp.tanh(jnp.dot(xcb, wa_ref[n], preferred_element_type=F32) + 0.5 * ba_ref[:, cs])
        ti = jnp.tanh(jnp.dot(xcb, wi_ref[n], preferred_element_type=F32) + 0.5 * bi_ref[:, cs])
        lam = lam_ref[:, cs]
        softplus = jnp.maximum(-lam, 0.0) + jnp.log1p(jnp.exp(-jnp.abs(lam)))
        log_a = (tr + 1.0) * (-0.5 * RG_C * softplus)
        a = jnp.exp(log_a)
        z = -jnp.tanh(log_a) * (a * a + 1.0)
        mult = jnp.where(z == 0.0, 0.0, z * lax.rsqrt(z))
        if pos0 <= 0:
            mult = jnp.where(first, 1.0, mult)
        a_ref[:, cs] = a
        u_ref[:, cs] = mult * ((0.5 * ti + 0.5) * xc)

    for c in range(0, d, SCAN_LANES):
        cs = slice(c, c + SCAN_LANES)

        def group(gi, hprev, cs=cs):
            rows = pl.ds(pl.multiple_of(gi * SUBLANES, SUBLANES), SUBLANES)
            a_cum, h_loc = _scan8(a_ref[rows, cs], u_ref[rows, cs])
            h = h_loc + a_cum * hprev
            u_ref[rows, cs] = h
            return jnp.broadcast_to(h[SUBLANES - 1:SUBLANES, :], h.shape)

        hc_ref[:, cs] = lax.fori_loop(0, ts // SUBLANES, group, hc_ref[:, cs], unroll=2)

    hg_ref[0] = (u_ref[...] * gate_ref[0]).astype(hg_ref.dtype)

    @pl.when(t == pl.num_programs(1) - 1)
    def _():
        cnew_ref[0] = xpad_ref[ts + h_lo:ts + CONV_PAD, :]
        hlast_ref[0] = hc_ref[0:1, :]


def _rglru(gate, xb, conv_prev, h0, cw, cb, wa, wi, layer, ba, bi, lam, *, pos0):
    b, s, d = xb.shape
    ts = min(s, RG_ROWS)
    hist = CONV_W - 1
    vec = lambda a: a.reshape(1, d)
    tile = pl.BlockSpec((1, ts, d), lambda i, t: (i, t, 0))
    vec_spec = pl.BlockSpec((1, d), lambda i, t: (0, 0))
    hg, cnew, hlast = pl.pallas_call(
        functools.partial(_rglru_kernel, pos0=pos0),
        grid=(b, s // ts),
        in_specs=[
            tile, tile,
            pl.BlockSpec((1, hist, d), lambda i, t: (i, 0, 0)),
            pl.BlockSpec((1, 1, d), lambda i, t: (i, 0, 0)),
            pl.BlockSpec((CONV_W, d), lambda i, t: (0, 0)),
            vec_spec, _resident(wa, layer), _resident(wi, layer), vec_spec, vec_spec, vec_spec,
        ],
        out_specs=[
            tile,
            pl.BlockSpec((1, hist, d), lambda i, t: (i, 0, 0)),
            pl.BlockSpec((1, 1, d), lambda i, t: (i, 0, 0)),
        ],
        out_shape=[
            jax.ShapeDtypeStruct((b, s, d), BF16),
            jax.ShapeDtypeStruct((b, hist, d), F32),
            jax.ShapeDtypeStruct((b, 1, d), F32),
        ],
        scratch_shapes=[
            pltpu.VMEM((CONV_PAD + ts, d), F32),
            pltpu.VMEM((ts, d), F32),
            pltpu.VMEM((ts, d), F32),
            pltpu.VMEM((SUBLANES, d), F32),
        ],
        compiler_params=_cparams(2),
        name="rglru",
    )(xb, gate, conv_prev, h0, cw, vec(cb), wa, wi, vec(ba), vec(bi), vec(lam))
    return hg, cnew, hlast.reshape(b, d)


def _build_bias(bt_ref, bias_ref):
    hb, tq, w = bias_ref.shape
    wb = bt_ref.shape[2]
    qi = lax.broadcasted_iota(jnp.int32, (tq, w), 0)
    kj = lax.broadcasted_iota(jnp.int32, (tq, w), 1)
    lo = (qi // CHUNK) * CHUNK
    band = (kj >= lo) & (kj < lo + BAND)
    for hh in range(hb):
        bias = pltpu.roll(jnp.broadcast_to(bt_ref[hh], (tq, wb)), 0, axis=1,
                          stride=1, stride_axis=0)[:, :w]
        bias_ref[hh] = jnp.where(band, bias * LOG2E, NEG)


def _softmax_pv(q, k, v, bias, scale):
    sc = lax.dot_general(q, k, (((1,), (1,)), ((), ())), preferred_element_type=F32)
    sc = sc * (scale * LOG2E) + bias
    m = jnp.max(sc, axis=-1, keepdims=True)
    p = jnp.exp2(sc - m)
    l = jnp.sum(p, axis=-1, keepdims=True)
    return jnp.dot(p.astype(BF16), v, preferred_element_type=F32) / l


def _attn_direct_kernel(q_ref, k_ref, v_ref, bt_ref, o_ref, bias_ref, *, scale):
    b = pl.program_id(1)
    t = pl.program_id(2)
    hb, tq, w = bias_ref.shape
    dh = q_ref.shape[2] // hb

    @pl.when((b == 0) & (t == 0))
    def _():
        _build_bias(bt_ref, bias_ref)

    def tile(row0, wp):
        for hh in range(hb):
            cs = slice(hh * dh, (hh + 1) * dh)
            o = _softmax_pv(q_ref[0, :, cs], k_ref[0, pl.ds(row0, wp), cs],
                            v_ref[0, pl.ds(row0, wp), cs], bias_ref[hh, :, w - wp:], scale)
            o_ref[0, :, cs] = o.astype(o_ref.dtype)

    n_early = LEFT // tq
    for e in range(n_early):
        pl.when(t == e)(functools.partial(tile, 0, (e + 1) * tq))

    @pl.when(t >= n_early)
    def _():
        tile(pl.multiple_of(t * tq - LEFT, tq), w)


def _attn_staged_kernel(q_ref, k_ref, v_ref, kc_ref, vc_ref, bt_ref, o_ref,
                        bias_ref, kp_ref, vp_ref, *, scale):
    b = pl.program_id(1)
    t = pl.program_id(2)
    hb, tq, w = bias_ref.shape
    s = k_ref.shape[1]
    rows, dh = kp_ref.shape[1:]

    @pl.when((b == 0) & (t == 0))
    def _():
        _build_bias(bt_ref, bias_ref)

    @pl.when(t == 0)
    def _():
        for hh in range(hb):
            cs = slice(hh * dh, (hh + 1) * dh)
            kp_ref[hh, 0:LEFT, :] = kc_ref[0, :, cs].astype(BF16)
            vp_ref[hh, 0:LEFT, :] = vc_ref[0, :, cs].astype(BF16)
            kp_ref[hh, LEFT:LEFT + s, :] = k_ref[0, :, cs]
            vp_ref[hh, LEFT:LEFT + s, :] = v_ref[0, :, cs]
            if rows > LEFT + s:
                kp_ref[hh, LEFT + s:, :] = jnp.zeros((rows - LEFT - s, dh), BF16)
                vp_ref[hh, LEFT + s:, :] = jnp.zeros((rows - LEFT - s, dh), BF16)

    q0 = pl.multiple_of(t * tq, tq)
    for hh in range(hb):
        cs = slice(hh * dh, (hh + 1) * dh)
        o = _softmax_pv(q_ref[0, :, cs], kp_ref[hh, pl.ds(q0, w), :], vp_ref[hh, pl.ds(q0, w), :],
                        bias_ref[hh], scale)
        o_ref[0, :, cs] = o.astype(o_ref.dtype)


def _bias_row(rel_table, wb):
    far = rel_table[2 * REL_CLIP]
    n_left = LEFT - REL_CLIP
    n_right = wb - n_left - (2 * REL_CLIP + 1)
    row = jnp.concatenate([
        jnp.broadcast_to(far, (n_left,) + far.shape), rel_table[::-1],
        jnp.broadcast_to(far, (n_right,) + far.shape)], axis=0)
    return row.T.reshape(rel_table.shape[1], 1, wb)


def _attention(q, k, v, k_cache, v_cache, rel_table):
    b, s, d = q.shape
    dh = d // N_HEADS
    tq = min(s, ATTN_ROWS)
    hb = ATTN_HEADS if s > tq else N_HEADS
    w = -(-(tq + LEFT) // LANES) * LANES
    wb = max(w, LEFT + REL_CLIP + 2 * LANES)
    qspec = pl.BlockSpec((1, tq, dh * hb), lambda h, i, t: (i, t, h))
    kspec = pl.BlockSpec((1, s, dh * hb), lambda h, i, t: (i, 0, h))
    bspec = pl.BlockSpec((hb, 1, wb), lambda h, i, t: (h, 0, 0))
    common = dict(
        grid=(N_HEADS // hb, b, s // tq),
        out_specs=qspec,
        out_shape=jax.ShapeDtypeStruct((b, s, d), BF16),
        compiler_params=_cparams(3),
    )
    bias_scratch = pltpu.VMEM((hb, tq, w), F32)
    bias_rows = _bias_row(rel_table, wb)
    if k_cache is None:
        assert LEFT % tq == 0 and tq % LANES == 0 and s % tq == 0
        return pl.pallas_call(
            functools.partial(_attn_direct_kernel, scale=dh ** -0.5),
            in_specs=[qspec, kspec, kspec, bspec],
            scratch_shapes=[bias_scratch],
            name="band_attn", **common,
        )(q, k, v, bias_rows)
    assert k_cache.shape[1] == LEFT, "the past band buffer holds exactly LEFT rows"
    rows = (s - tq) + w
    cspec = pl.BlockSpec((1, LEFT, dh * hb), lambda h, i, t: (i, 0, h))
    return pl.pallas_call(
        functools.partial(_attn_staged_kernel, scale=dh ** -0.5),
        in_specs=[qspec, kspec, kspec, cspec, cspec, bspec],
        scratch_shapes=[bias_scratch, pltpu.VMEM((hb, rows, dh), BF16),
                        pltpu.VMEM((hb, rows, dh), BF16)],
        name="band_attn_past", **common,
    )(q, k, v, k_cache, v_cache, bias_rows)


def _trunk(x, mod, pos0, conv_state, rnn_state, k_cache, v_cache, p):
    b, s, d = x.shape
    depth = mod.shape[0]
    n_a = p['rg_w_in'].shape[0]
    mods = [[m.reshape(b, 1, d) for m in jnp.split(mod[layer], 6, axis=-1)] for layer in range(depth)]
    mix_spec = lambda l: (p['g_mix'][l], mods[l][0], mods[l][1], BF16)
    kv_spec = (p['g_kv'], None, None, BF16)

    def next_specs(l):
        if l == depth:
            return [(p['g_final'], None, None, F32)]
        return [mix_spec(l)] + ([kv_spec] if l == n_a else [])

    normed = _norm_rows(x, next_specs(0))
    conv_out, rnn_out = [], []
    k_new = v_new = k_tail = v_tail = None
    for layer in range(depth):
        _, _, g1, sh2, sc2, g2 = mods[layer]
        if layer == n_a:
            k_new, v_new, k_tail, v_tail = _kv_proj(normed[1], p['w_k'], p['w_v'])
        if layer < n_a:
            gate, xb = _mm(normed[0], [p['rg_w_gate'], p['rg_w_in']], layer,
                           gelu=(True, False), out_dtypes=(F32, F32))
            cprev = jnp.zeros((b, CONV_W - 1, d), F32) if conv_state is None else conv_state[layer]
            h0 = jnp.zeros((b, 1, d), F32) if rnn_state is None else rnn_state[layer].reshape(b, 1, d)
            mix, cst, hst = _rglru(gate, xb, cprev, h0, p['rg_conv_w'][layer], p['rg_conv_b'][layer],
                                   p['rg_w_a'], p['rg_w_i'], layer, p['rg_b_a'][layer],
                                   p['rg_b_i'][layer], p['rg_lambda'][layer], pos0=pos0)
            conv_out.append(cst)
            rnn_out.append(hst)
            w_out, w_layer = p['rg_w_out'], layer
        else:
            bl = layer - n_a
            q, = _mm(normed[0], [p['w_q']], bl, gelu=(False,), out_dtypes=(BF16,))
            mix = _attention(q, k_new, v_new, k_cache, v_cache, p['rel_bias'][bl])
            w_out, w_layer = p['w_o'], bl
        x, hf = _proj_res(mix, w_out, w_layer, x, g1, [(p['g_ffn'][layer], sh2, sc2, BF16)])
        last = layer == depth - 1
        outs = _ffn(hf, x, g2, p['ffn_w1'], p['ffn_w3'], p['ffn_w2'], layer,
                    next_specs(layer + 1), emit_x=not last)
        if last:
            x = outs[0]
        else:
            x, normed = outs[0], outs[1:]
    return x, jnp.stack(conv_out), jnp.stack(rnn_out), k_tail, v_tail


def kernel(x_prompt, x_sample, c_prompt, c_sample, state_conv, state_rnn, cache_k, cache_v, ada_w, ada_b, g_mix, g_ffn, rg_w_in, rg_w_gate, rg_conv_w, rg_conv_b, rg_w_a, rg_b_a, rg_w_i, rg_b_i, rg_lambda, rg_w_out, g_kv, w_k, w_v, w_q, w_o, rel_bias, ffn_w1, ffn_w3, ffn_w2, g_final):
    p = {
        'g_mix': g_mix, 'g_ffn': g_ffn, 'g_kv': g_kv, 'g_final': g_final,
        'rg_w_gate': rg_w_gate.astype(BF16), 'rg_w_in': rg_w_in.astype(BF16),
        'rg_conv_w': rg_conv_w, 'rg_conv_b': rg_conv_b,
        'rg_w_a': (0.5 * rg_w_a).astype(BF16), 'rg_b_a': rg_b_a,
        'rg_w_i': (0.5 * rg_w_i).astype(BF16), 'rg_b_i': rg_b_i,
        'rg_lambda': rg_lambda, 'rg_w_out': rg_w_out.astype(BF16),
        'w_k': w_k.astype(BF16), 'w_v': w_v.astype(BF16),
        'w_q': w_q.astype(BF16), 'w_o': w_o.astype(BF16), 'rel_bias': rel_bias,
        'ffn_w1': ffn_w1.astype(BF16), 'ffn_w3': ffn_w3.astype(BF16), 'ffn_w2': ffn_w2.astype(BF16),
    }
    nbp = c_prompt.shape[0]
    mod = _ada(jnp.concatenate([c_prompt, c_sample], axis=0), ada_w, ada_b)

    y_p, conv_p, rnn_p, k_p, v_p = _trunk(x_prompt, mod[:, :nbp], 0, None, None, None, None, p)
    nbs, n_past, n_heads, dh = cache_k.shape
    y_s, conv_s, rnn_s, k_s, v_s = _trunk(
        x_sample, mod[:, nbp:], PAST_LEN, state_conv, state_rnn,
        cache_k.reshape(nbs, n_past, n_heads * dh), cache_v.reshape(nbs, n_past, n_heads * dh), p)
    return (y_p, y_s, conv_p, rnn_p, k_p, v_p, conv_s, rnn_s, k_s, v_s)
```

```python
import functools

import jax
import jax.numpy as jnp
from jax import lax
from jax.experimental import pallas as pl
from jax.experimental.pallas import tpu as pltpu

F32 = jnp.float32
BF16 = jnp.bfloat16

CHUNK = 64
LEFT = 512
BAND = LEFT + CHUNK
REL_CLIP = 128
RG_C = 8.0
RG_BLOCKS = 8
CONV_W = 4
N_HEADS = 16
EPS = 1e-6
PAST_LEN = 2048
NEG = -1e30
LOG2E = 1.4426950408889634

LANES = 128
SUBLANES = 8
VMEM_LIMIT = 56 * 1024 * 1024

BM = 512
NCHUNK = 1024
ADA_COLS = 1024
NORM_ROWS = 64
PROJ_NORM_ROWS = 128
CONV_PAD = 8
FFN_ROWS = 1024
FFN_COLS = 512
FFN_REFILL_STEP = 2
SCAN_LANES = 1024
RG_ROWS = 256
ATTN_ROWS = 256
ATTN_HEADS = 8


def _cparams(n_axes):
    return pltpu.CompilerParams(
        dimension_semantics=("arbitrary",) * n_axes, vmem_limit_bytes=VMEM_LIMIT)


def _row_tiles(s, bm=BM):
    ts = min(s, bm)
    return bm // ts, ts, s // ts


def _resident(w, layer=None):
    if layer is None:
        shape, index = w.shape, (0,) * w.ndim
    else:
        shape, index = (None,) + w.shape[1:], (layer,) + (0,) * (w.ndim - 1)
    return pl.BlockSpec(shape, lambda *idx: index, pipeline_mode=pl.Buffered(1))


def _norm_inputs(specs, nb, spt):
    args, in_specs = [], []
    for g, sh, sc, _ in specs:
        d = g.shape[-1]
        args.append(g.reshape(1, d))
        in_specs.append(pl.BlockSpec((1, d), lambda *idx: (0, 0)))
        if sh is not None:
            mod_spec = pl.BlockSpec((nb, 1, d), lambda *idx: (idx[0] // spt, 0, 0))
            args += [sh, sc]
            in_specs += [mod_spec, mod_spec]
    return args, in_specs


def _take_norm_refs(refs, mods):
    out, pos = [], 0
    for has_mod in mods:
        if has_mod:
            out.append(refs[pos:pos + 3])
            pos += 3
        else:
            out.append((refs[pos], None, None))
            pos += 1
    return out, refs[pos:]


def _emit_norms(y, b, rows, norm_refs, out_refs):
    yn = y * lax.rsqrt(jnp.mean(y * y, axis=-1, keepdims=True) + EPS)
    for (g_ref, sh_ref, sc_ref), o_ref in zip(norm_refs, out_refs):
        v = yn * g_ref[...]
        if sh_ref is not None:
            v = v * (1.0 + sc_ref[b]) + sh_ref[b]
        o_ref[b, rows, :] = v.astype(o_ref.dtype)


def _norm_out(specs, b, s, d, nb, ts, spt):
    out_specs = [pl.BlockSpec((nb, ts, d), lambda *idx: (idx[0] // spt, idx[0] % spt, 0))
                 for _ in specs]
    out_shape = [jax.ShapeDtypeStruct((b, s, d), spec[3]) for spec in specs]
    return out_specs, out_shape


def _row_loop(nb, ts, body):
    rc = min(ts, NORM_ROWS)
    nr = ts // rc

    def step(it, carry):
        rows = pl.ds(pl.multiple_of((it % nr) * rc, rc), rc)
        flat = pl.ds(pl.multiple_of(it * rc, rc), rc)
        body(it // nr, rows, flat)
        return carry

    lax.fori_loop(0, nb * nr, step, 0)


def _ada_kernel(c_ref, w_ref, b_ref, o_ref):
    c = c_ref[...]
    cs = (c * jax.nn.sigmoid(c)).astype(BF16)
    o_ref[0] = jnp.dot(cs, w_ref[0].astype(BF16), preferred_element_type=F32) + b_ref[0]


def _ada(c, ada_w, ada_b):
    nl, d, n = ada_w.shape
    bt = c.shape[0]
    return pl.pallas_call(
        _ada_kernel,
        grid=(nl, n // ADA_COLS),
        in_specs=[
            pl.BlockSpec((bt, d), lambda l, j: (0, 0)),
            pl.BlockSpec((1, d, ADA_COLS), lambda l, j: (l, 0, j)),
            pl.BlockSpec((1, 1, ADA_COLS), lambda l, j: (l, 0, j)),
        ],
        out_specs=pl.BlockSpec((1, bt, ADA_COLS), lambda l, j: (l, 0, j)),
        out_shape=jax.ShapeDtypeStruct((nl, bt, n), F32),
        compiler_params=_cparams(2),
        name="ada_mod",
    )(c, ada_w, ada_b.reshape(nl, 1, n))


def _norm_kernel(x_ref, *refs, mods):
    norm_refs, out_refs = _take_norm_refs(refs, mods)
    nb, ts, _ = x_ref.shape

    def body(b, rows, flat):
        _emit_norms(x_ref[b, rows, :], b, rows, norm_refs, out_refs)

    _row_loop(nb, ts, body)


def _norm_rows(x, specs):
    b, s, d = x.shape
    nb, ts, spt = _row_tiles(s)
    n_args, n_specs = _norm_inputs(specs, nb, spt)
    out_specs, out_shape = _norm_out(specs, b, s, d, nb, ts, spt)
    return pl.pallas_call(
        functools.partial(_norm_kernel, mods=tuple(sp[1] is not None for sp in specs)),
        grid=(b * s // BM,),
        in_specs=[pl.BlockSpec((nb, ts, d), lambda i: (i // spt, i % spt, 0))] + n_specs,
        out_specs=out_specs,
        out_shape=out_shape,
        compiler_params=_cparams(1),
        name="norm_rows",
    )(x, *n_args)


def _mm_kernel(a_ref, *refs, gelu):
    n_w = len(gelu)
    nb, ts, k = a_ref.shape
    a = a_ref[...].reshape(nb * ts, k)
    for w_ref, o_ref, use_gelu in zip(refs[:n_w], refs[n_w:], gelu):
        for c in range(0, w_ref.shape[1], NCHUNK):
            acc = jnp.dot(a, w_ref[:, c:c + NCHUNK], preferred_element_type=F32)
            if use_gelu:
                acc = jax.nn.gelu(acc)
            o_ref[:, :, c:c + NCHUNK] = acc.reshape(nb, ts, NCHUNK).astype(o_ref.dtype)


def _mm(a, ws, layer, *, gelu, out_dtypes):
    b, s, k = a.shape
    nb, ts, spt = _row_tiles(s)
    row = lambda i: (i // spt, i % spt, 0)
    return pl.pallas_call(
        functools.partial(_mm_kernel, gelu=tuple(gelu)),
        grid=(b * s // BM,),
        in_specs=[pl.BlockSpec((nb, ts, k), row)] + [_resident(w, layer) for w in ws],
        out_specs=[pl.BlockSpec((nb, ts, w.shape[2]), row) for w in ws],
        out_shape=[jax.ShapeDtypeStruct((b, s, w.shape[2]), dt) for w, dt in zip(ws, out_dtypes)],
        compiler_params=_cparams(1),
        name="proj",
    )(a, *ws)


def _kv_kernel(a_ref, wk_ref, wv_ref, kb_ref, vb_ref, kt_ref, vt_ref, *, spt):
    nb, ts, kdim = a_ref.shape
    dh = kt_ref.shape[3]

    def run(with_tail):
        a = a_ref[...].reshape(nb * ts, kdim)
        for w_ref, ob_ref, ot_ref in ((wk_ref, kb_ref, kt_ref), (wv_ref, vb_ref, vt_ref)):
            for c in range(0, w_ref.shape[1], NCHUNK):
                acc = jnp.dot(a, w_ref[:, c:c + NCHUNK], preferred_element_type=F32)
                acc = acc.reshape(nb, ts, NCHUNK)
                ob_ref[:, :, c:c + NCHUNK] = acc.astype(ob_ref.dtype)
                if with_tail:
                    for h in range(NCHUNK // dh):
                        ot_ref[:, :, c // dh + h, :] = acc[:, :, h * dh:(h + 1) * dh]

    if spt == 1:
        run(True)
    else:
        is_tail = pl.program_id(0) % spt == spt - 1
        pl.when(is_tail)(functools.partial(run, True))
        pl.when(jnp.logical_not(is_tail))(functools.partial(run, False))


def _kv_proj(a, wk, wv):
    b, s, k = a.shape
    n = wk.shape[1]
    dh = n // N_HEADS
    nb, ts, spt = _row_tiles(s)
    assert ts == min(LEFT, s), "the returned K/V rows are exactly the last row tile of a sequence"
    row = lambda i: (i // spt, i % spt, 0)
    full = pl.BlockSpec((nb, ts, n), row)
    tail = pl.BlockSpec((nb, ts, N_HEADS, dh), lambda i: (i // spt, 0, 0, 0))
    return pl.pallas_call(
        functools.partial(_kv_kernel, spt=spt),
        grid=(b * s // BM,),
        in_specs=[pl.BlockSpec((nb, ts, k), row), _resident(wk), _resident(wv)],
        out_specs=[full, full, tail, tail],
        out_shape=[jax.ShapeDtypeStruct((b, s, n), BF16)] * 2
        + [jax.ShapeDtypeStruct((b, ts, N_HEADS, dh), F32)] * 2,
        compiler_params=_cparams(1),
        name="kv_proj",
    )(a, wk, wv)


def _proj_res_kernel(a_ref, w_ref, x_ref, gate_ref, *refs, mods):
    norm_refs, (o_ref, *hn_refs) = _take_norm_refs(refs, mods)
    nb, ts, k = a_ref.shape
    a = a_ref[...].reshape(nb * ts, k)
    for c in range(0, w_ref.shape[1], NCHUNK):
        cols = slice(c, c + NCHUNK)
        acc = jnp.dot(a, w_ref[:, cols], preferred_element_type=F32)
        o_ref[:, :, cols] = x_ref[:, :, cols] + gate_ref[:, :, cols] * acc.reshape(nb, ts, NCHUNK)
    rc = min(ts, PROJ_NORM_ROWS)
    for b in range(nb):
        for r in range(0, ts, rc):
            _emit_norms(o_ref[b, r:r + rc, :], b, slice(r, r + rc), norm_refs, hn_refs)


def _proj_res(a, w, layer, x, gate, specs):
    b, s, d = x.shape
    k = a.shape[2]
    nb, ts, spt = _row_tiles(s)
    row = lambda i: (i // spt, i % spt, 0)
    n_args, n_specs = _norm_inputs(specs, nb, spt)
    out_specs, out_shape = _norm_out(specs, b, s, d, nb, ts, spt)
    return pl.pallas_call(
        functools.partial(_proj_res_kernel, mods=tuple(sp[1] is not None for sp in specs)),
        grid=(b * s // BM,),
        in_specs=[pl.BlockSpec((nb, ts, k), row), _resident(w, layer),
                  pl.BlockSpec((nb, ts, d), row),
                  pl.BlockSpec((nb, 1, d), lambda i: (i // spt, 0, 0))] + n_specs,
        out_specs=[pl.BlockSpec((nb, ts, d), row)] + out_specs,
        out_shape=[jax.ShapeDtypeStruct((b, s, d), F32)] + out_shape,
        compiler_params=_cparams(1),
        name="proj_res",
    )(a, w, x, gate, *n_args)


def _ffn_kernel(hf_ref, gate_ref, x_hbm, w1_hbm, w3_hbm, w2_hbm, *refs, mods, emit_x, layer, spt):
    norm_refs, rest = _take_norm_refs(refs, mods)
    n_out = emit_x + len(mods)
    out_hbm, (acc_ref, x_buf, *rest) = rest[:n_out], rest[n_out:]
    norm_bufs, (w1_buf, w3_buf, w2_buf, wsem, xsem, osem) = rest[:len(mods)], rest[len(mods):]
    out_bufs = ([x_buf] if emit_x else []) + list(norm_bufs)
    i = pl.program_id(0)
    nt = pl.num_programs(0)
    nb, ts, d = x_buf.shape
    bf = w1_buf.shape[2]
    nj = w1_hbm.shape[2] // bf
    n_steps = nt * nj

    def weight_copies(blk, slot):
        cols = pl.ds(pl.multiple_of(blk * bf, bf), bf)
        return (pltpu.make_async_copy(w1_hbm.at[layer, :, cols], w1_buf.at[slot], wsem.at[0, slot]),
                pltpu.make_async_copy(w3_hbm.at[layer, :, cols], w3_buf.at[slot], wsem.at[1, slot]),
                pltpu.make_async_copy(w2_hbm.at[layer, cols, :], w2_buf.at[slot], wsem.at[2, slot]))

    def tile_of(ref, t):
        rows = pl.ds(0, ts) if spt == 1 else pl.ds(pl.multiple_of((t % spt) * ts, ts), ts)
        return ref.at[pl.ds(pl.multiple_of((t // spt) * nb, nb), nb), rows, :]

    def x_copy(t):
        return pltpu.make_async_copy(tile_of(x_hbm, t), x_buf, xsem.at[0])

    def out_copies(t):
        return [pltpu.make_async_copy(buf, tile_of(hbm, t), osem.at[k])
                for k, (buf, hbm) in enumerate(zip(out_bufs, out_hbm))]

    @pl.when(i == 0)
    def _():
        acc_ref[...] = jnp.zeros(acc_ref.shape, F32)
        for copy in weight_copies(0, 0):
            copy.start()
        x_copy(0).start()

    def step(j, carry):
        n = i * nj + j
        slot = n % 2

        @pl.when(n + 1 < n_steps)
        def _():
            for copy in weight_copies((j + 1) % nj, 1 - slot):
                copy.start()

        @pl.when((j == FFN_REFILL_STEP) & (i > 0))
        def _():
            for copy in out_copies(i - 1):
                copy.wait()
            x_copy(i).start()

        for copy in weight_copies(j, slot):
            copy.wait()
        hf = hf_ref[...].reshape(nb * ts, d)
        h1 = jnp.dot(hf, w1_buf[slot], preferred_element_type=F32)
        h3 = jnp.dot(hf, w3_buf[slot], preferred_element_type=F32)
        act = (h1 * jax.nn.sigmoid(h1) * h3).astype(BF16)
        acc_ref[...] += jnp.dot(act, w2_buf[slot], preferred_element_type=F32)
        return carry

    lax.fori_loop(0, nj, step, 0)
    x_copy(i).wait()

    def body(b, rows, flat):
        ff = acc_ref[flat, :]
        acc_ref[flat, :] = jnp.zeros_like(ff)
        y = x_buf[b, rows, :] + gate_ref[b] * ff
        if emit_x:
            x_buf[b, rows, :] = y
        _emit_norms(y, b, rows, norm_refs, norm_bufs)

    _row_loop(nb, ts, body)
    for copy in out_copies(i):
        copy.start()

    @pl.when(i == nt - 1)
    def _():
        for copy in out_copies(i):
            copy.wait()


def _ffn(hf, x, gate, w1, w3, w2, layer, specs, *, emit_x):
    b, s, d = x.shape
    nb, ts, spt = _row_tiles(s, FFN_ROWS)
    assert w1.shape[2] // FFN_COLS > FFN_REFILL_STEP
    n_args, n_specs = _norm_inputs(specs, nb, spt)
    out_shape = [jax.ShapeDtypeStruct((b, s, d), sp[3]) for sp in specs]
    if emit_x:
        out_shape = [jax.ShapeDtypeStruct((b, s, d), F32)] + out_shape
    hbm = pl.BlockSpec(memory_space=pl.ANY)
    return pl.pallas_call(
        functools.partial(_ffn_kernel, mods=tuple(sp[1] is not None for sp in specs),
                          emit_x=emit_x, layer=layer, spt=spt),
        grid=(b * s // FFN_ROWS,),
        in_specs=[
            pl.BlockSpec((nb, ts, d), lambda i: (i // spt, i % spt, 0)),
            pl.BlockSpec((nb, 1, d), lambda i: (i // spt, 0, 0)),
            hbm, hbm, hbm, hbm,
        ] + n_specs,
        out_specs=[hbm] * len(out_shape),
        out_shape=out_shape,
        scratch_shapes=[pltpu.VMEM((FFN_ROWS, d), F32), pltpu.VMEM((nb, ts, d), F32)]
        + [pltpu.VMEM((nb, ts, d), sp[3]) for sp in specs]
        + [pltpu.VMEM((2, d, FFN_COLS), BF16), pltpu.VMEM((2, d, FFN_COLS), BF16),
           pltpu.VMEM((2, FFN_COLS, d), BF16), pltpu.SemaphoreType.DMA((3, 2)),
           pltpu.SemaphoreType.DMA((1,)), pltpu.SemaphoreType.DMA((len(out_shape),))],
        compiler_params=_cparams(1),
        name="ffn",
    )(hf, gate, x, w1, w3, w2, *n_args)


def _scan8(a, b):
    row = lax.broadcasted_iota(jnp.int32, a.shape, 0)
    for k in (1, 2, 4):
        a_sh = jnp.where(row >= k, pltpu.roll(a, k, axis=0), 1.0)
        b_sh = jnp.where(row >= k, pltpu.roll(b, k, axis=0), 0.0)
        b = a * b_sh + b
        a = a * a_sh
    return a, b


def _rglru_kernel(xb_ref, gate_ref, cprev_ref, h0_ref, cw_ref, cb_ref, wa_ref, wi_ref,
                  ba_ref, bi_ref, lam_ref, hg_ref, cnew_ref, hlast_ref,
                  xpad_ref, a_ref, u_ref, hc_ref, *, pos0):
    t = pl.program_id(1)
    ts, d = xb_ref.shape[1:]
    bw = d // RG_BLOCKS
    hist = CONV_W - 1
    h_lo = CONV_PAD - hist

    @pl.when(t == 0)
    def _():
        xpad_ref[h_lo:CONV_PAD, :] = cprev_ref[0]
        hc_ref[...] = jnp.broadcast_to(h0_ref[0], hc_ref.shape)

    @pl.when(t > 0)
    def _():
        xpad_ref[h_lo:CONV_PAD, :] = xpad_ref[ts + h_lo:ts + CONV_PAD, :]

    xpad_ref[CONV_PAD:CONV_PAD + ts, :] = xb_ref[0]

    first = (pos0 + t * ts + lax.broadcasted_iota(jnp.int32, (ts, bw), 0)) == 0
    for n in range(RG_BLOCKS):
        cs = slice(n * bw, (n + 1) * bw)
        xp = xpad_ref[:, cs]
        xc = cb_ref[:, cs] + xp * cw_ref[0:1, cs]
        for k in range(1, CONV_W):
            xc = pltpu.roll(xc, 1, axis=0) + xp * cw_ref[k:k + 1, cs]
        xc = xc[CONV_PAD:, :]
        xcb = xc.astype(BF16)
        tr = jnp.tanh(jnp.dot(xcb, wa_ref[n], preferred_element_type=F32) + 0.5 * ba_ref[:, cs])
        ti = jnp.tanh(jnp.dot(xcb, wi_ref[n], preferred_element_type=F32) + 0.5 * bi_ref[:, cs])
        lam = lam_ref[:, cs]
        softplus = jnp.maximum(-lam, 0.0) + jnp.log1p(jnp.exp(-jnp.abs(lam)))
        log_a = (tr + 1.0) * (-0.5 * RG_C * softplus)
        a = jnp.exp(log_a)
        z = -jnp.tanh(log_a) * (a * a + 1.0)
        mult = jnp.where(z == 0.0, 0.0, z * lax.rsqrt(z))
        if pos0 <= 0:
            mult = jnp.where(first, 1.0, mult)
        a_ref[:, cs] = a
        u_ref[:, cs] = mult * ((0.5 * ti + 0.5) * xc)

    for c in range(0, d, SCAN_LANES):
        cs = slice(c, c + SCAN_LANES)

        def group(gi, hprev, cs=cs):
            rows = pl.ds(pl.multiple_of(gi * SUBLANES, SUBLANES), SUBLANES)
            a_cum, h_loc = _scan8(a_ref[rows, cs], u_ref[rows, cs])
            h = h_loc + a_cum * hprev
            u_ref[rows, cs] = h
            return jnp.broadcast_to(h[SUBLANES - 1:SUBLANES, :], h.shape)

        hc_ref[:, cs] = lax.fori_loop(0, ts // SUBLANES, group, hc_ref[:, cs], unroll=4)

    hg_ref[0] = (u_ref[...] * gate_ref[0]).astype(hg_ref.dtype)

    @pl.when(t == pl.num_programs(1) - 1)
    def _():
        cnew_ref[0] = xpad_ref[ts + h_lo:ts + CONV_PAD, :]
        hlast_ref[0] = hc_ref[0:1, :]


def _rglru(gate, xb, conv_prev, h0, cw, cb, wa, wi, layer, ba, bi, lam, *, pos0):
    b, s, d = xb.shape
    ts = min(s, RG_ROWS)
    hist = CONV_W - 1
    vec = lambda a: a.reshape(1, d)
    tile = pl.BlockSpec((1, ts, d), lambda i, t: (i, t, 0))
    vec_spec = pl.BlockSpec((1, d), lambda i, t: (0, 0))
    hg, cnew, hlast = pl.pallas_call(
        functools.partial(_rglru_kernel, pos0=pos0),
        grid=(b, s // ts),
        in_specs=[
            tile, tile,
            pl.BlockSpec((1, hist, d), lambda i, t: (i, 0, 0)),
            pl.BlockSpec((1, 1, d), lambda i, t: (i, 0, 0)),
            pl.BlockSpec((CONV_W, d), lambda i, t: (0, 0)),
            vec_spec, _resident(wa, layer), _resident(wi, layer), vec_spec, vec_spec, vec_spec,
        ],
        out_specs=[
            tile,
            pl.BlockSpec((1, hist, d), lambda i, t: (i, 0, 0)),
            pl.BlockSpec((1, 1, d), lambda i, t: (i, 0, 0)),
        ],
        out_shape=[
            jax.ShapeDtypeStruct((b, s, d), BF16),
            jax.ShapeDtypeStruct((b, hist, d), F32),
            jax.ShapeDtypeStruct((b, 1, d), F32),
        ],
        scratch_shapes=[
            pltpu.VMEM((CONV_PAD + ts, d), F32),
            pltpu.VMEM((ts, d), F32),
            pltpu.VMEM((ts, d), F32),
            pltpu.VMEM((SUBLANES, d), F32),
        ],
        compiler_params=_cparams(2),
        name="rglru",
    )(xb, gate, conv_prev, h0, cw, vec(cb), wa, wi, vec(ba), vec(bi), vec(lam))
    return hg, cnew, hlast.reshape(b, d)


def _build_bias(bt_ref, bias_ref):
    hb, tq, w = bias_ref.shape
    wb = bt_ref.shape[2]
    qi = lax.broadcasted_iota(jnp.int32, (tq, w), 0)
    kj = lax.broadcasted_iota(jnp.int32, (tq, w), 1)
    lo = (qi // CHUNK) * CHUNK
    band = (kj >= lo) & (kj < lo + BAND)
    for hh in range(hb):
        bias = pltpu.roll(jnp.broadcast_to(bt_ref[hh], (tq, wb)), 0, axis=1,
                          stride=1, stride_axis=0)[:, :w]
        bias_ref[hh] = jnp.where(band, bias * LOG2E, NEG)


def _scores(q, k, bias, scale):
    sc = lax.dot_general(q, k, (((1,), (1,)), ((), ())), preferred_element_type=F32)
    return sc * (scale * LOG2E) + bias


def _probs(sc):
    m = jnp.max(sc, axis=-1, keepdims=True)
    p = jnp.exp2(sc - m)
    return p.astype(BF16), jnp.sum(p, axis=-1, keepdims=True)


def _softmax_out(sc, v):
    p, l = _probs(sc)
    return jnp.dot(p, v, preferred_element_type=F32) / l


def _softmax_pv(q, k, v, bias, scale):
    return _softmax_out(_scores(q, k, bias, scale), v)


def _attn_direct_kernel(q_ref, k_ref, v_ref, bt_ref, o_ref, bias_ref, *, scale):
    b = pl.program_id(1)
    t = pl.program_id(2)
    hb, tq, w = bias_ref.shape
    dh = q_ref.shape[2] // hb

    @pl.when((b == 0) & (t == 0))
    def _():
        _build_bias(bt_ref, bias_ref)

    def tile(row0, wp):
        heads = [slice(hh * dh, (hh + 1) * dh) for hh in range(hb)]
        scores = [_scores(q_ref[0, :, cs], k_ref[0, pl.ds(row0, wp), cs],
                          bias_ref[hh, :, w - wp:], scale) for hh, cs in enumerate(heads)]
        probs = [_probs(sc) for sc in scores]
        for (p, l), cs in zip(probs, heads):
            o = jnp.dot(p, v_ref[0, pl.ds(row0, wp), cs], preferred_element_type=F32) / l
            o_ref[0, :, cs] = o.astype(o_ref.dtype)

    n_early = LEFT // tq
    for e in range(n_early):
        pl.when(t == e)(functools.partial(tile, 0, (e + 1) * tq))

    @pl.when(t >= n_early)
    def _():
        tile(pl.multiple_of(t * tq - LEFT, tq), w)


def _attn_staged_kernel(q_ref, k_ref, v_ref, kc_ref, vc_ref, bt_ref, o_ref,
                        bias_ref, kp_ref, vp_ref, *, scale):
    b = pl.program_id(1)
    t = pl.program_id(2)
    hb, tq, w = bias_ref.shape
    s = k_ref.shape[1]
    rows, dh = kp_ref.shape[1:]

    @pl.when((b == 0) & (t == 0))
    def _():
        _build_bias(bt_ref, bias_ref)

    @pl.when(t == 0)
    def _():
        for hh in range(hb):
            cs = slice(hh * dh, (hh + 1) * dh)
            kp_ref[hh, 0:LEFT, :] = kc_ref[0, :, cs].astype(BF16)
            vp_ref[hh, 0:LEFT, :] = vc_ref[0, :, cs].astype(BF16)
            kp_ref[hh, LEFT:LEFT + s, :] = k_ref[0, :, cs]
            vp_ref[hh, LEFT:LEFT + s, :] = v_ref[0, :, cs]
            if rows > LEFT + s:
                kp_ref[hh, LEFT + s:, :] = jnp.zeros((rows - LEFT - s, dh), BF16)
                vp_ref[hh, LEFT + s:, :] = jnp.zeros((rows - LEFT - s, dh), BF16)

    q0 = pl.multiple_of(t * tq, tq)
    for hh in range(hb):
        cs = slice(hh * dh, (hh + 1) * dh)
        o = _softmax_pv(q_ref[0, :, cs], kp_ref[hh, pl.ds(q0, w), :], vp_ref[hh, pl.ds(q0, w), :],
                        bias_ref[hh], scale)
        o_ref[0, :, cs] = o.astype(o_ref.dtype)


def _bias_row(rel_table, wb):
    far = rel_table[2 * REL_CLIP]
    n_left = LEFT - REL_CLIP
    n_right = wb - n_left - (2 * REL_CLIP + 1)
    row = jnp.concatenate([
        jnp.broadcast_to(far, (n_left,) + far.shape), rel_table[::-1],
        jnp.broadcast_to(far, (n_right,) + far.shape)], axis=0)
    return row.T.reshape(rel_table.shape[1], 1, wb)


def _attention(q, k, v, k_cache, v_cache, rel_table):
    b, s, d = q.shape
    dh = d // N_HEADS
    tq = min(s, ATTN_ROWS)
    hb = ATTN_HEADS if s > tq else N_HEADS
    w = -(-(tq + LEFT) // LANES) * LANES
    wb = max(w, LEFT + REL_CLIP + 2 * LANES)
    qspec = pl.BlockSpec((1, tq, dh * hb), lambda h, i, t: (i, t, h))
    kspec = pl.BlockSpec((1, s, dh * hb), lambda h, i, t: (i, 0, h))
    bspec = pl.BlockSpec((hb, 1, wb), lambda h, i, t: (h, 0, 0))
    common = dict(
        grid=(N_HEADS // hb, b, s // tq),
        out_specs=qspec,
        out_shape=jax.ShapeDtypeStruct((b, s, d), BF16),
        compiler_params=_cparams(3),
    )
    bias_scratch = pltpu.VMEM((hb, tq, w), F32)
    bias_rows = _bias_row(rel_table, wb)
    if k_cache is None:
        assert LEFT % tq == 0 and tq % LANES == 0 and s % tq == 0
        return pl.pallas_call(
            functools.partial(_attn_direct_kernel, scale=dh ** -0.5),
            in_specs=[qspec, kspec, kspec, bspec],
            scratch_shapes=[bias_scratch],
            name="band_attn", **common,
        )(q, k, v, bias_rows)
    assert k_cache.shape[1] == LEFT, "the past band buffer holds exactly LEFT rows"
    rows = (s - tq) + w
    cspec = pl.BlockSpec((1, LEFT, dh * hb), lambda h, i, t: (i, 0, h))
    return pl.pallas_call(
        functools.partial(_attn_staged_kernel, scale=dh ** -0.5),
        in_specs=[qspec, kspec, kspec, cspec, cspec, bspec],
        scratch_shapes=[bias_scratch, pltpu.VMEM((hb, rows, dh), BF16),
                        pltpu.VMEM((hb, rows, dh), BF16)],
        name="band_attn_past", **common,
    )(q, k, v, k_cache, v_cache, bias_rows)


def _trunk(x, mod, pos0, conv_state, rnn_state, k_cache, v_cache, p):
    b, s, d = x.shape
    depth = mod.shape[0]
    n_a = p['rg_w_in'].shape[0]
    mods = [[m.reshape(b, 1, d) for m in jnp.split(mod[layer], 6, axis=-1)] for layer in range(depth)]
    mix_spec = lambda l: (p['g_mix'][l], mods[l][0], mods[l][1], BF16)
    kv_spec = (p['g_kv'], None, None, BF16)

    def next_specs(l):
        if l == depth:
            return [(p['g_final'], None, None, F32)]
        return [mix_spec(l)] + ([kv_spec] if l == n_a else [])

    normed = _norm_rows(x, next_specs(0))
    conv_out, rnn_out = [], []
    k_new = v_new = k_tail = v_tail = None
    for layer in range(depth):
        _, _, g1, sh2, sc2, g2 = mods[layer]
        if layer == n_a:
            k_new, v_new, k_tail, v_tail = _kv_proj(normed[1], p['w_k'], p['w_v'])
        if layer < n_a:
            gate, xb = _mm(normed[0], [p['rg_w_gate'], p['rg_w_in']], layer,
                           gelu=(True, False), out_dtypes=(F32, F32))
            cprev = jnp.zeros((b, CONV_W - 1, d), F32) if conv_state is None else conv_state[layer]
            h0 = jnp.zeros((b, 1, d), F32) if rnn_state is None else rnn_state[layer].reshape(b, 1, d)
            mix, cst, hst = _rglru(gate, xb, cprev, h0, p['rg_conv_w'][layer], p['rg_conv_b'][layer],
                                   p['rg_w_a'], p['rg_w_i'], layer, p['rg_b_a'][layer],
                                   p['rg_b_i'][layer], p['rg_lambda'][layer], pos0=pos0)
            conv_out.append(cst)
            rnn_out.append(hst)
            w_out, w_layer = p['rg_w_out'], layer
        else:
            bl = layer - n_a
            q, = _mm(normed[0], [p['w_q']], bl, gelu=(False,), out_dtypes=(BF16,))
            mix = _attention(q, k_new, v_new, k_cache, v_cache, p['rel_bias'][bl])
            w_out, w_layer = p['w_o'], bl
        x, hf = _proj_res(mix, w_out, w_layer, x, g1, [(p['g_ffn'][layer], sh2, sc2, BF16)])
        last = layer == depth - 1
        outs = _ffn(hf, x, g2, p['ffn_w1'], p['ffn_w3'], p['ffn_w2'], layer,
                    next_specs(layer + 1), emit_x=not last)
        if last:
            x = outs[0]
        else:
            x, normed = outs[0], outs[1:]
    return x, jnp.stack(conv_out), jnp.stack(rnn_out), k_tail, v_tail


def kernel(x_prompt, x_sample, c_prompt, c_sample, state_conv, state_rnn, cache_k, cache_v, ada_w, ada_b, g_mix, g_ffn, rg_w_in, rg_w_gate, rg_conv_w, rg_conv_b, rg_w_a, rg_b_a, rg_w_i, rg_b_i, rg_lambda, rg_w_out, g_kv, w_k, w_v, w_q, w_o, rel_bias, ffn_w1, ffn_w3, ffn_w2, g_final):
    p = {
        'g_mix': g_mix, 'g_ffn': g_ffn, 'g_kv': g_kv, 'g_final': g_final,
        'rg_w_gate': rg_w_gate.astype(BF16), 'rg_w_in': rg_w_in.astype(BF16),
        'rg_conv_w': rg_conv_w, 'rg_conv_b': rg_conv_b,
        'rg_w_a': (0.5 * rg_w_a).astype(BF16), 'rg_b_a': rg_b_a,
        'rg_w_i': (0.5 * rg_w_i).astype(BF16), 'rg_b_i': rg_b_i,
        'rg_lambda': rg_lambda, 'rg_w_out': rg_w_out.astype(BF16),
        'w_k': w_k.astype(BF16), 'w_v': w_v.astype(BF16),
        'w_q': w_q.astype(BF16), 'w_o': w_o.astype(BF16), 'rel_bias': rel_bias,
        'ffn_w1': ffn_w1.astype(BF16), 'ffn_w3': ffn_w3.astype(BF16), 'ffn_w2': ffn_w2.astype(BF16),
    }
    nbp = c_prompt.shape[0]
    mod = _ada(jnp.concatenate([c_prompt, c_sample], axis=0), ada_w, ada_b)

    y_p, conv_p, rnn_p, k_p, v_p = _trunk(x_prompt, mod[:, :nbp], 0, None, None, None, None, p)
    nbs, n_past, n_heads, dh = cache_k.shape
    y_s, conv_s, rnn_s, k_s, v_s = _trunk(
        x_sample, mod[:, nbp:], PAST_LEN, state_conv, state_rnn,
        cache_k.reshape(nbs, n_past, n_heads * dh), cache_v.reshape(nbs, n_past, n_heads * dh), p)
    return (y_p, y_s, conv_p, rnn_p, k_p, v_p, conv_s, rnn_s, k_s, v_s)
```

```python
import functools

import jax
import jax.numpy as jnp
from jax import lax
from jax.experimental import pallas as pl
from jax.experimental.pallas import tpu as pltpu

F32 = jnp.float32
BF16 = jnp.bfloat16

CHUNK = 64
LEFT = 512
BAND = LEFT + CHUNK
REL_CLIP = 128
RG_C = 8.0
RG_BLOCKS = 8
CONV_W = 4
N_HEADS = 16
EPS = 1e-6
PAST_LEN = 2048
NEG = -1e30
LOG2E = 1.4426950408889634

LANES = 128
SUBLANES = 8
V7X_VMEM_BYTES = 64 * 1024 * 1024
VMEM_LIMIT = V7X_VMEM_BYTES - 8 * 1024 * 1024

BM = 512
NCHUNK = 1024
ADA_COLS = 1024
NORM_ROWS = 64
PROJ_NORM_ROWS = 128
CONV_PAD = 8
FFN_ROWS = 1024
FFN_COLS = 512
FFN_REFILL_STEP = 2
SCAN_LANES = 1024
RG_ROWS = 512
ATTN_ROWS = 256
ATTN_HEADS = 8


def _cparams(n_axes):
    return pltpu.CompilerParams(
        dimension_semantics=("arbitrary",) * n_axes, vmem_limit_bytes=VMEM_LIMIT)


def _row_tiles(s, bm=BM):
    ts = min(s, bm)
    return bm // ts, ts, s // ts


def _resident(w, layer=None):
    if layer is None:
        shape, index = w.shape, (0,) * w.ndim
    else:
        shape, index = (None,) + w.shape[1:], (layer,) + (0,) * (w.ndim - 1)
    return pl.BlockSpec(shape, lambda *idx: index, pipeline_mode=pl.Buffered(1))


def _norm_inputs(specs, nb, spt):
    args, in_specs = [], []
    for g, sh, sc, _ in specs:
        d = g.shape[-1]
        args.append(g.reshape(1, d))
        in_specs.append(pl.BlockSpec((1, d), lambda *idx: (0, 0)))
        if sh is not None:
            mod_spec = pl.BlockSpec((nb, 1, d), lambda *idx: (idx[0] // spt, 0, 0))
            args += [sh, sc]
            in_specs += [mod_spec, mod_spec]
    return args, in_specs


def _take_norm_refs(refs, mods):
    out, pos = [], 0
    for has_mod in mods:
        if has_mod:
            out.append(refs[pos:pos + 3])
            pos += 3
        else:
            out.append((refs[pos], None, None))
            pos += 1
    return out, refs[pos:]


def _emit_norms(y, b, rows, norm_refs, out_refs):
    yn = y * lax.rsqrt(jnp.mean(y * y, axis=-1, keepdims=True) + EPS)
    for (g_ref, sh_ref, sc_ref), o_ref in zip(norm_refs, out_refs):
        v = yn * g_ref[...]
        if sh_ref is not None:
            v = v * (1.0 + sc_ref[b]) + sh_ref[b]
        o_ref[b, rows, :] = v.astype(o_ref.dtype)


def _norm_out(specs, b, s, d, nb, ts, spt):
    out_specs = [pl.BlockSpec((nb, ts, d), lambda *idx: (idx[0] // spt, idx[0] % spt, 0))
                 for _ in specs]
    out_shape = [jax.ShapeDtypeStruct((b, s, d), spec[3]) for spec in specs]
    return out_specs, out_shape


def _row_loop(nb, ts, body):
    rc = min(ts, NORM_ROWS)
    nr = ts // rc

    def step(it, carry):
        rows = pl.ds(pl.multiple_of((it % nr) * rc, rc), rc)
        flat = pl.ds(pl.multiple_of(it * rc, rc), rc)
        body(it // nr, rows, flat)
        return carry

    lax.fori_loop(0, nb * nr, step, 0)


def _ada_kernel(c_ref, w_ref, b_ref, o_ref):
    c = c_ref[...]
    cs = (c * jax.nn.sigmoid(c)).astype(BF16)
    o_ref[0] = jnp.dot(cs, w_ref[0].astype(BF16), preferred_element_type=F32) + b_ref[0]


def _ada(c, ada_w, ada_b):
    nl, d, n = ada_w.shape
    bt = c.shape[0]
    return pl.pallas_call(
        _ada_kernel,
        grid=(nl, n // ADA_COLS),
        in_specs=[
            pl.BlockSpec((bt, d), lambda l, j: (0, 0)),
            pl.BlockSpec((1, d, ADA_COLS), lambda l, j: (l, 0, j)),
            pl.BlockSpec((1, 1, ADA_COLS), lambda l, j: (l, 0, j)),
        ],
        out_specs=pl.BlockSpec((1, bt, ADA_COLS), lambda l, j: (l, 0, j)),
        out_shape=jax.ShapeDtypeStruct((nl, bt, n), F32),
        compiler_params=_cparams(2),
        name="ada_mod",
    )(c, ada_w, ada_b.reshape(nl, 1, n))


def _norm_kernel(x_ref, *refs, mods):
    norm_refs, out_refs = _take_norm_refs(refs, mods)
    nb, ts, _ = x_ref.shape

    def body(b, rows, flat):
        _emit_norms(x_ref[b, rows, :], b, rows, norm_refs, out_refs)

    _row_loop(nb, ts, body)


def _norm_rows(x, specs):
    b, s, d = x.shape
    nb, ts, spt = _row_tiles(s)
    n_args, n_specs = _norm_inputs(specs, nb, spt)
    out_specs, out_shape = _norm_out(specs, b, s, d, nb, ts, spt)
    return pl.pallas_call(
        functools.partial(_norm_kernel, mods=tuple(sp[1] is not None for sp in specs)),
        grid=(b * s // BM,),
        in_specs=[pl.BlockSpec((nb, ts, d), lambda i: (i // spt, i % spt, 0))] + n_specs,
        out_specs=out_specs,
        out_shape=out_shape,
        compiler_params=_cparams(1),
        name="norm_rows",
    )(x, *n_args)


def _mm_kernel(a_ref, *refs, gelu):
    n_w = len(gelu)
    nb, ts, k = a_ref.shape
    a = a_ref[...].reshape(nb * ts, k)
    for w_ref, o_ref, use_gelu in zip(refs[:n_w], refs[n_w:], gelu):
        for c in range(0, w_ref.shape[1], NCHUNK):
            acc = jnp.dot(a, w_ref[:, c:c + NCHUNK], preferred_element_type=F32)
            if use_gelu:
                acc = jax.nn.gelu(acc)
            o_ref[:, :, c:c + NCHUNK] = acc.reshape(nb, ts, NCHUNK).astype(o_ref.dtype)


def _mm(a, ws, layer, *, gelu, out_dtypes):
    b, s, k = a.shape
    nb, ts, spt = _row_tiles(s)
    row = lambda i: (i // spt, i % spt, 0)
    return pl.pallas_call(
        functools.partial(_mm_kernel, gelu=tuple(gelu)),
        grid=(b * s // BM,),
        in_specs=[pl.BlockSpec((nb, ts, k), row)] + [_resident(w, layer) for w in ws],
        out_specs=[pl.BlockSpec((nb, ts, w.shape[2]), row) for w in ws],
        out_shape=[jax.ShapeDtypeStruct((b, s, w.shape[2]), dt) for w, dt in zip(ws, out_dtypes)],
        compiler_params=_cparams(1),
        name="proj",
    )(a, *ws)


def _kv_kernel(a_ref, wk_ref, wv_ref, kb_ref, vb_ref, kt_ref, vt_ref, *, spt):
    nb, ts, kdim = a_ref.shape
    dh = kt_ref.shape[3]

    def run(with_tail):
        a = a_ref[...].reshape(nb * ts, kdim)
        for w_ref, ob_ref, ot_ref in ((wk_ref, kb_ref, kt_ref), (wv_ref, vb_ref, vt_ref)):
            for c in range(0, w_ref.shape[1], NCHUNK):
                acc = jnp.dot(a, w_ref[:, c:c + NCHUNK], preferred_element_type=F32)
                acc = acc.reshape(nb, ts, NCHUNK)
                ob_ref[:, :, c:c + NCHUNK] = acc.astype(ob_ref.dtype)
                if with_tail:
                    for h in range(NCHUNK // dh):
                        ot_ref[:, :, c // dh + h, :] = acc[:, :, h * dh:(h + 1) * dh]

    if spt == 1:
        run(True)
    else:
        is_tail = pl.program_id(0) % spt == spt - 1
        pl.when(is_tail)(functools.partial(run, True))
        pl.when(jnp.logical_not(is_tail))(functools.partial(run, False))


def _kv_proj(a, wk, wv):
    b, s, k = a.shape
    n = wk.shape[1]
    dh = n // N_HEADS
    nb, ts, spt = _row_tiles(s)
    assert ts == min(LEFT, s), "the returned K/V rows are exactly the last row tile of a sequence"
    row = lambda i: (i // spt, i % spt, 0)
    full = pl.BlockSpec((nb, ts, n), row)
    tail = pl.BlockSpec((nb, ts, N_HEADS, dh), lambda i: (i // spt, 0, 0, 0))
    return pl.pallas_call(
        functools.partial(_kv_kernel, spt=spt),
        grid=(b * s // BM,),
        in_specs=[pl.BlockSpec((nb, ts, k), row), _resident(wk), _resident(wv)],
        out_specs=[full, full, tail, tail],
        out_shape=[jax.ShapeDtypeStruct((b, s, n), BF16)] * 2
        + [jax.ShapeDtypeStruct((b, ts, N_HEADS, dh), F32)] * 2,
        compiler_params=_cparams(1),
        name="kv_proj",
    )(a, wk, wv)


def _proj_res_kernel(a_ref, w_ref, x_ref, gate_ref, *refs, mods):
    norm_refs, (o_ref, *hn_refs) = _take_norm_refs(refs, mods)
    nb, ts, k = a_ref.shape
    a = a_ref[...].reshape(nb * ts, k)
    for c in range(0, w_ref.shape[1], NCHUNK):
        cols = slice(c, c + NCHUNK)
        acc = jnp.dot(a, w_ref[:, cols], preferred_element_type=F32)
        o_ref[:, :, cols] = x_ref[:, :, cols] + gate_ref[:, :, cols] * acc.reshape(nb, ts, NCHUNK)
    rc = min(ts, PROJ_NORM_ROWS)
    for b in range(nb):
        for r in range(0, ts, rc):
            _emit_norms(o_ref[b, r:r + rc, :], b, slice(r, r + rc), norm_refs, hn_refs)


def _proj_res(a, w, layer, x, gate, specs):
    b, s, d = x.shape
    k = a.shape[2]
    nb, ts, spt = _row_tiles(s)
    row = lambda i: (i // spt, i % spt, 0)
    n_args, n_specs = _norm_inputs(specs, nb, spt)
    out_specs, out_shape = _norm_out(specs, b, s, d, nb, ts, spt)
    return pl.pallas_call(
        functools.partial(_proj_res_kernel, mods=tuple(sp[1] is not None for sp in specs)),
        grid=(b * s // BM,),
        in_specs=[pl.BlockSpec((nb, ts, k), row), _resident(w, layer),
                  pl.BlockSpec((nb, ts, d), row),
                  pl.BlockSpec((nb, 1, d), lambda i: (i // spt, 0, 0))] + n_specs,
        out_specs=[pl.BlockSpec((nb, ts, d), row)] + out_specs,
        out_shape=[jax.ShapeDtypeStruct((b, s, d), F32)] + out_shape,
        compiler_params=_cparams(1),
        name="proj_res",
    )(a, w, x, gate, *n_args)


def _ffn_kernel(hf_ref, gate_ref, x_hbm, w1_hbm, w3_hbm, w2_hbm, *refs, mods, emit_x, layer, spt):
    norm_refs, rest = _take_norm_refs(refs, mods)
    n_out = emit_x + len(mods)
    out_hbm, (acc_ref, x_buf, *rest) = rest[:n_out], rest[n_out:]
    norm_bufs, (w1_buf, w3_buf, w2_buf, wsem, xsem, osem) = rest[:len(mods)], rest[len(mods):]
    out_bufs = ([x_buf] if emit_x else []) + list(norm_bufs)
    i = pl.program_id(0)
    nt = pl.num_programs(0)
    nb, ts, d = x_buf.shape
    bf = w1_buf.shape[2]
    nj = w1_hbm.shape[2] // bf
    n_steps = nt * nj

    def weight_copies(blk, slot):
        cols = pl.ds(pl.multiple_of(blk * bf, bf), bf)
        return (pltpu.make_async_copy(w1_hbm.at[layer, :, cols], w1_buf.at[slot], wsem.at[0, slot]),
                pltpu.make_async_copy(w3_hbm.at[layer, :, cols], w3_buf.at[slot], wsem.at[1, slot]),
                pltpu.make_async_copy(w2_hbm.at[layer, cols, :], w2_buf.at[slot], wsem.at[2, slot]))

    def tile_of(ref, t):
        rows = pl.ds(0, ts) if spt == 1 else pl.ds(pl.multiple_of((t % spt) * ts, ts), ts)
        return ref.at[pl.ds(pl.multiple_of((t // spt) * nb, nb), nb), rows, :]

    def x_copy(t):
        return pltpu.make_async_copy(tile_of(x_hbm, t), x_buf, xsem.at[0])

    def out_copies(t):
        return [pltpu.make_async_copy(buf, tile_of(hbm, t), osem.at[k])
                for k, (buf, hbm) in enumerate(zip(out_bufs, out_hbm))]

    @pl.when(i == 0)
    def _():
        acc_ref[...] = jnp.zeros(acc_ref.shape, F32)
        for copy in weight_copies(0, 0):
            copy.start()
        x_copy(0).start()

    def step(j, carry):
        n = i * nj + j
        slot = n % 2

        @pl.when(n + 1 < n_steps)
        def _():
            for copy in weight_copies((j + 1) % nj, 1 - slot):
                copy.start()

        @pl.when((j == FFN_REFILL_STEP) & (i > 0))
        def _():
            for copy in out_copies(i - 1):
                copy.wait()
            x_copy(i).start()

        for copy in weight_copies(j, slot):
            copy.wait()
        hf = hf_ref[...].reshape(nb * ts, d)
        h1 = jnp.dot(hf, w1_buf[slot], preferred_element_type=F32)
        h3 = jnp.dot(hf, w3_buf[slot], preferred_element_type=F32)
        act = (h1 * jax.nn.sigmoid(h1) * h3).astype(BF16)
        acc_ref[...] += jnp.dot(act, w2_buf[slot], preferred_element_type=F32)
        return carry

    lax.fori_loop(0, nj, step, 0)
    x_copy(i).wait()

    def body(b, rows, flat):
        ff = acc_ref[flat, :]
        acc_ref[flat, :] = jnp.zeros_like(ff)
        y = x_buf[b, rows, :] + gate_ref[b] * ff
        if emit_x:
            x_buf[b, rows, :] = y
        _emit_norms(y, b, rows, norm_refs, norm_bufs)

    _row_loop(nb, ts, body)
    for copy in out_copies(i):
        copy.start()

    @pl.when(i == nt - 1)
    def _():
        for copy in out_copies(i):
            copy.wait()


def _ffn(hf, x, gate, w1, w3, w2, layer, specs, *, emit_x):
    b, s, d = x.shape
    nb, ts, spt = _row_tiles(s, FFN_ROWS)
    assert w1.shape[2] // FFN_COLS > FFN_REFILL_STEP
    n_args, n_specs = _norm_inputs(specs, nb, spt)
    out_shape = [jax.ShapeDtypeStruct((b, s, d), sp[3]) for sp in specs]
    if emit_x:
        out_shape = [jax.ShapeDtypeStruct((b, s, d), F32)] + out_shape
    hbm = pl.BlockSpec(memory_space=pl.ANY)
    return pl.pallas_call(
        functools.partial(_ffn_kernel, mods=tuple(sp[1] is not None for sp in specs),
                          emit_x=emit_x, layer=layer, spt=spt),
        grid=(b * s // FFN_ROWS,),
        in_specs=[
            pl.BlockSpec((nb, ts, d), lambda i: (i // spt, i % spt, 0)),
            pl.BlockSpec((nb, 1, d), lambda i: (i // spt, 0, 0)),
            hbm, hbm, hbm, hbm,
        ] + n_specs,
        out_specs=[hbm] * len(out_shape),
        out_shape=out_shape,
        scratch_shapes=[pltpu.VMEM((FFN_ROWS, d), F32), pltpu.VMEM((nb, ts, d), F32)]
        + [pltpu.VMEM((nb, ts, d), sp[3]) for sp in specs]
        + [pltpu.VMEM((2, d, FFN_COLS), BF16), pltpu.VMEM((2, d, FFN_COLS), BF16),
           pltpu.VMEM((2, FFN_COLS, d), BF16), pltpu.SemaphoreType.DMA((3, 2)),
           pltpu.SemaphoreType.DMA((1,)), pltpu.SemaphoreType.DMA((len(out_shape),))],
        compiler_params=_cparams(1),
        name="ffn",
    )(hf, gate, x, w1, w3, w2, *n_args)


def _scan8(a, b):
    row = lax.broadcasted_iota(jnp.int32, a.shape, 0)
    for k in (1, 2, 4):
        a_sh = jnp.where(row >= k, pltpu.roll(a, k, axis=0), 1.0)
        b_sh = jnp.where(row >= k, pltpu.roll(b, k, axis=0), 0.0)
        b = a * b_sh + b
        a = a * a_sh
    return a, b


def _rglru_kernel(xb_ref, gate_ref, cprev_ref, h0_ref, cw_ref, cb_ref, wa_ref, wi_ref,
                  ba_ref, bi_ref, lam_ref, hg_ref, cnew_ref, hlast_ref,
                  xpad_ref, a_ref, u_ref, hc_ref, *, pos0):
    t = pl.program_id(1)
    ts, d = xb_ref.shape[1:]
    bw = d // RG_BLOCKS
    hist = CONV_W - 1
    h_lo = CONV_PAD - hist

    @pl.when(t == 0)
    def _():
        xpad_ref[h_lo:CONV_PAD, :] = cprev_ref[0]
        hc_ref[...] = jnp.broadcast_to(h0_ref[0], hc_ref.shape)

    @pl.when(t > 0)
    def _():
        xpad_ref[h_lo:CONV_PAD, :] = xpad_ref[ts + h_lo:ts + CONV_PAD, :]

    xpad_ref[CONV_PAD:CONV_PAD + ts, :] = xb_ref[0]

    first = (pos0 + t * ts + lax.broadcasted_iota(jnp.int32, (ts, bw), 0)) == 0
    for n in range(RG_BLOCKS):
        cs = slice(n * bw, (n + 1) * bw)
        xp = xpad_ref[:, cs]
        xc = cb_ref[:, cs] + xp * cw_ref[0:1, cs]
        for k in range(1, CONV_W):
            xc = pltpu.roll(xc, 1, axis=0) + xp * cw_ref[k:k + 1, cs]
        xc = xc[CONV_PAD:, :]
        xcb = xc.astype(BF16)
        tr = jnp.tanh(jnp.dot(xcb, wa_ref[n], preferred_element_type=F32) + 0.5 * ba_ref[:, cs])
        ti = jnp.tanh(jnp.dot(xcb, wi_ref[n], preferred_element_type=F32) + 0.5 * bi_ref[:, cs])
        lam = lam_ref[:, cs]
        softplus = jnp.maximum(-lam, 0.0) + jnp.log1p(jnp.exp(-jnp.abs(lam)))
        log_a = (tr + 1.0) * (-0.5 * RG_C * softplus)
        a = jnp.exp(log_a)
        z = -jnp.tanh(log_a) * (a * a + 1.0)
        mult = jnp.where(z == 0.0, 0.0, z * lax.rsqrt(z))
        if pos0 <= 0:
            mult = jnp.where(first, 1.0, mult)
        a_ref[:, cs] = a
        u_ref[:, cs] = mult * ((0.5 * ti + 0.5) * xc)

    for c in range(0, d, SCAN_LANES):
        cs = slice(c, c + SCAN_LANES)

        def group(gi, hprev, cs=cs):
            rows = pl.ds(pl.multiple_of(gi * SUBLANES, SUBLANES), SUBLANES)
            a_cum, h_loc = _scan8(a_ref[rows, cs], u_ref[rows, cs])
            h = h_loc + a_cum * hprev
            u_ref[rows, cs] = h
            return jnp.broadcast_to(h[SUBLANES - 1:SUBLANES, :], h.shape)

        hc_ref[:, cs] = lax.fori_loop(0, ts // SUBLANES, group, hc_ref[:, cs], unroll=4)

    hg_ref[0] = (u_ref[...] * gate_ref[0]).astype(hg_ref.dtype)

    @pl.when(t == pl.num_programs(1) - 1)
    def _():
        cnew_ref[0] = xpad_ref[ts + h_lo:ts + CONV_PAD, :]
        hlast_ref[0] = hc_ref[0:1, :]


def _rglru(gate, xb, conv_prev, h0, cw, cb, wa, wi, layer, ba, bi, lam, *, pos0):
    b, s, d = xb.shape
    ts = min(s, RG_ROWS)
    hist = CONV_W - 1
    vec = lambda a: a.reshape(1, d)
    tile = pl.BlockSpec((1, ts, d), lambda i, t: (i, t, 0))
    vec_spec = pl.BlockSpec((1, d), lambda i, t: (0, 0))
    hg, cnew, hlast = pl.pallas_call(
        functools.partial(_rglru_kernel, pos0=pos0),
        grid=(b, s // ts),
        in_specs=[
            tile, tile,
            pl.BlockSpec((1, hist, d), lambda i, t: (i, 0, 0)),
            pl.BlockSpec((1, 1, d), lambda i, t: (i, 0, 0)),
            pl.BlockSpec((CONV_W, d), lambda i, t: (0, 0)),
            vec_spec, _resident(wa, layer), _resident(wi, layer), vec_spec, vec_spec, vec_spec,
        ],
        out_specs=[
            tile,
            pl.BlockSpec((1, hist, d), lambda i, t: (i, 0, 0)),
            pl.BlockSpec((1, 1, d), lambda i, t: (i, 0, 0)),
        ],
        out_shape=[
            jax.ShapeDtypeStruct((b, s, d), BF16),
            jax.ShapeDtypeStruct((b, hist, d), F32),
            jax.ShapeDtypeStruct((b, 1, d), F32),
        ],
        scratch_shapes=[
            pltpu.VMEM((CONV_PAD + ts, d), F32),
            pltpu.VMEM((ts, d), F32),
            pltpu.VMEM((ts, d), F32),
            pltpu.VMEM((SUBLANES, d), F32),
        ],
        compiler_params=_cparams(2),
        name="rglru",
    )(xb, gate, conv_prev, h0, cw, vec(cb), wa, wi, vec(ba), vec(bi), vec(lam))
    return hg, cnew, hlast.reshape(b, d)


def _build_bias(bt_ref, bias_ref):
    hb, tq, w = bias_ref.shape
    wb = bt_ref.shape[2]
    qi = lax.broadcasted_iota(jnp.int32, (tq, w), 0)
    kj = lax.broadcasted_iota(jnp.int32, (tq, w), 1)
    lo = (qi // CHUNK) * CHUNK
    band = (kj >= lo) & (kj < lo + BAND)
    for hh in range(hb):
        bias = pltpu.roll(jnp.broadcast_to(bt_ref[hh], (tq, wb)), 0, axis=1,
                          stride=1, stride_axis=0)[:, :w]
        bias_ref[hh] = jnp.where(band, bias * LOG2E, NEG)


def _scores(q, k, bias, scale):
    sc = lax.dot_general(q, k, (((1,), (1,)), ((), ())), preferred_element_type=F32)
    return sc * (scale * LOG2E) + bias


def _softmax_out(sc, v):
    m = jnp.max(sc, axis=-1, keepdims=True)
    p = jnp.exp2(sc - m)
    l = jnp.sum(p, axis=-1, keepdims=True)
    return jnp.dot(p.astype(BF16), v, preferred_element_type=F32) / l


def _attn_direct_kernel(q_ref, k_ref, v_ref, bt_ref, o_ref, bias_ref, *, scale):
    b = pl.program_id(1)
    t = pl.program_id(2)
    hb, tq, w = bias_ref.shape
    dh = q_ref.shape[2] // hb

    @pl.when((b == 0) & (t == 0))
    def _():
        _build_bias(bt_ref, bias_ref)

    def tile(row0, wp):
        heads = [slice(hh * dh, (hh + 1) * dh) for hh in range(hb)]
        scores = [_scores(q_ref[0, :, cs], k_ref[0, pl.ds(row0, wp), cs],
                          bias_ref[hh, :, w - wp:], scale) for hh, cs in enumerate(heads)]
        for sc, cs in zip(scores, heads):
            o = _softmax_out(sc, v_ref[0, pl.ds(row0, wp), cs])
            o_ref[0, :, cs] = o.astype(o_ref.dtype)

    n_early = LEFT // tq
    for e in range(n_early):
        pl.when(t == e)(functools.partial(tile, 0, (e + 1) * tq))

    @pl.when(t >= n_early)
    def _():
        tile(pl.multiple_of(t * tq - LEFT, tq), w)


def _attn_staged_kernel(q_ref, k_ref, v_ref, kc_ref, vc_ref, bt_ref, o_ref,
                        bias_ref, kp_ref, vp_ref, *, scale):
    b = pl.program_id(1)
    t = pl.program_id(2)
    hb, tq, w = bias_ref.shape
    s = k_ref.shape[1]
    rows, dh = kp_ref.shape[1:]

    @pl.when((b == 0) & (t == 0))
    def _():
        _build_bias(bt_ref, bias_ref)

    @pl.when(t == 0)
    def _():
        for hh in range(hb):
            cs = slice(hh * dh, (hh + 1) * dh)
            kp_ref[hh, 0:LEFT, :] = kc_ref[0, :, cs].astype(BF16)
            vp_ref[hh, 0:LEFT, :] = vc_ref[0, :, cs].astype(BF16)
            kp_ref[hh, LEFT:LEFT + s, :] = k_ref[0, :, cs]
            vp_ref[hh, LEFT:LEFT + s, :] = v_ref[0, :, cs]
            if rows > LEFT + s:
                kp_ref[hh, LEFT + s:, :] = jnp.zeros((rows - LEFT - s, dh), BF16)
                vp_ref[hh, LEFT + s:, :] = jnp.zeros((rows - LEFT - s, dh), BF16)

    q0 = pl.multiple_of(t * tq, tq)
    heads = [slice(hh * dh, (hh + 1) * dh) for hh in range(hb)]
    scores = [_scores(q_ref[0, :, cs], kp_ref[hh, pl.ds(q0, w), :], bias_ref[hh], scale)
              for hh, cs in enumerate(heads)]
    for hh, (sc, cs) in enumerate(zip(scores, heads)):
        o = _softmax_out(sc, vp_ref[hh, pl.ds(q0, w), :])
        o_ref[0, :, cs] = o.astype(o_ref.dtype)


def _bias_row(rel_table, wb):
    far = rel_table[2 * REL_CLIP]
    n_left = LEFT - REL_CLIP
    n_right = wb - n_left - (2 * REL_CLIP + 1)
    row = jnp.concatenate([
        jnp.broadcast_to(far, (n_left,) + far.shape), rel_table[::-1],
        jnp.broadcast_to(far, (n_right,) + far.shape)], axis=0)
    return row.T.reshape(rel_table.shape[1], 1, wb)


def _attention(q, k, v, k_cache, v_cache, rel_table):
    b, s, d = q.shape
    dh = d // N_HEADS
    tq = min(s, ATTN_ROWS)
    hb = ATTN_HEADS if s > tq else N_HEADS
    w = -(-(tq + LEFT) // LANES) * LANES
    wb = max(w, LEFT + REL_CLIP + 2 * LANES)
    qspec = pl.BlockSpec((1, tq, dh * hb), lambda h, i, t: (i, t, h))
    kspec = pl.BlockSpec((1, s, dh * hb), lambda h, i, t: (i, 0, h))
    bspec = pl.BlockSpec((hb, 1, wb), lambda h, i, t: (h, 0, 0))
    common = dict(
        grid=(N_HEADS // hb, b, s // tq),
        out_specs=qspec,
        out_shape=jax.ShapeDtypeStruct((b, s, d), BF16),
        compiler_params=_cparams(3),
    )
    bias_scratch = pltpu.VMEM((hb, tq, w), F32)
    bias_rows = _bias_row(rel_table, wb)
    if k_cache is None:
        assert LEFT % tq == 0 and tq % LANES == 0 and s % tq == 0
        return pl.pallas_call(
            functools.partial(_attn_direct_kernel, scale=dh ** -0.5),
            in_specs=[qspec, kspec, kspec, bspec],
            scratch_shapes=[bias_scratch],
            name="band_attn", **common,
        )(q, k, v, bias_rows)
    assert k_cache.shape[1] == LEFT, "the past band buffer holds exactly LEFT rows"
    rows = (s - tq) + w
    cspec = pl.BlockSpec((1, LEFT, dh * hb), lambda h, i, t: (i, 0, h))
    return pl.pallas_call(
        functools.partial(_attn_staged_kernel, scale=dh ** -0.5),
        in_specs=[qspec, kspec, kspec, cspec, cspec, bspec],
        scratch_shapes=[bias_scratch, pltpu.VMEM((hb, rows, dh), BF16),
                        pltpu.VMEM((hb, rows, dh), BF16)],
        name="band_attn_past", **common,
    )(q, k, v, k_cache, v_cache, bias_rows)


def _trunk(x, mod, pos0, conv_state, rnn_state, k_cache, v_cache, p):
    b, s, d = x.shape
    depth = mod.shape[0]
    n_a = p['rg_w_in'].shape[0]
    mods = [[m.reshape(b, 1, d) for m in jnp.split(mod[layer], 6, axis=-1)] for layer in range(depth)]
    mix_spec = lambda l: (p['g_mix'][l], mods[l][0], mods[l][1], BF16)
    kv_spec = (p['g_kv'], None, None, BF16)

    def next_specs(l):
        if l == depth:
            return [(p['g_final'], None, None, F32)]
        return [mix_spec(l)] + ([kv_spec] if l == n_a else [])

    normed = _norm_rows(x, next_specs(0))
    conv_out, rnn_out = [], []
    k_new = v_new = k_tail = v_tail = None
    for layer in range(depth):
        _, _, g1, sh2, sc2, g2 = mods[layer]
        if layer == n_a:
            k_new, v_new, k_tail, v_tail = _kv_proj(normed[1], p['w_k'], p['w_v'])
        if layer < n_a:
            gate, xb = _mm(normed[0], [p['rg_w_gate'], p['rg_w_in']], layer,
                           gelu=(True, False), out_dtypes=(F32, F32))
            cprev = jnp.zeros((b, CONV_W - 1, d), F32) if conv_state is None else conv_state[layer]
            h0 = jnp.zeros((b, 1, d), F32) if rnn_state is None else rnn_state[layer].reshape(b, 1, d)
            mix, cst, hst = _rglru(gate, xb, cprev, h0, p['rg_conv_w'][layer], p['rg_conv_b'][layer],
                                   p['rg_w_a'], p['rg_w_i'], layer, p['rg_b_a'][layer],
                                   p['rg_b_i'][layer], p['rg_lambda'][layer], pos0=pos0)
            conv_out.append(cst)
            rnn_out.append(hst)
            w_out, w_layer = p['rg_w_out'], layer
        else:
            bl = layer - n_a
            q, = _mm(normed[0], [p['w_q']], bl, gelu=(False,), out_dtypes=(BF16,))
            mix = _attention(q, k_new, v_new, k_cache, v_cache, p['rel_bias'][bl])
            w_out, w_layer = p['w_o'], bl
        x, hf = _proj_res(mix, w_out, w_layer, x, g1, [(p['g_ffn'][layer], sh2, sc2, BF16)])
        last = layer == depth - 1
        outs = _ffn(hf, x, g2, p['ffn_w1'], p['ffn_w3'], p['ffn_w2'], layer,
                    next_specs(layer + 1), emit_x=not last)
        if last:
            x = outs[0]
        else:
            x, normed = outs[0], outs[1:]
    return x, jnp.stack(conv_out), jnp.stack(rnn_out), k_tail, v_tail


def kernel(x_prompt, x_sample, c_prompt, c_sample, state_conv, state_rnn, cache_k, cache_v, ada_w, ada_b, g_mix, g_ffn, rg_w_in, rg_w_gate, rg_conv_w, rg_conv_b, rg_w_a, rg_b_a, rg_w_i, rg_b_i, rg_lambda, rg_w_out, g_kv, w_k, w_v, w_q, w_o, rel_bias, ffn_w1, ffn_w3, ffn_w2, g_final):
    p = {
        'g_mix': g_mix, 'g_ffn': g_ffn, 'g_kv': g_kv, 'g_final': g_final,
        'rg_w_gate': rg_w_gate.astype(BF16), 'rg_w_in': rg_w_in.astype(BF16),
        'rg_conv_w': rg_conv_w, 'rg_conv_b': rg_conv_b,
        'rg_w_a': (0.5 * rg_w_a).astype(BF16), 'rg_b_a': rg_b_a,
        'rg_w_i': (0.5 * rg_w_i).astype(BF16), 'rg_b_i': rg_b_i,
        'rg_lambda': rg_lambda, 'rg_w_out': rg_w_out.astype(BF16),
        'w_k': w_k.astype(BF16), 'w_v': w_v.astype(BF16),
        'w_q': w_q.astype(BF16), 'w_o': w_o.astype(BF16), 'rel_bias': rel_bias,
        'ffn_w1': ffn_w1.astype(BF16), 'ffn_w3': ffn_w3.astype(BF16), 'ffn_w2': ffn_w2.astype(BF16),
    }
    nbp = c_prompt.shape[0]
    mod = _ada(jnp.concatenate([c_prompt, c_sample], axis=0), ada_w, ada_b)

    y_p, conv_p, rnn_p, k_p, v_p = _trunk(x_prompt, mod[:, :nbp], 0, None, None, None, None, p)
    nbs, n_past, n_heads, dh = cache_k.shape
    y_s, conv_s, rnn_s, k_s, v_s = _trunk(
        x_sample, mod[:, nbp:], PAST_LEN, state_conv, state_rnn,
        cache_k.reshape(nbs, n_past, n_heads * dh), cache_v.reshape(nbs, n_past, n_heads * dh), p)
    return (y_p, y_s, conv_p, rnn_p, k_p, v_p, conv_s, rnn_s, k_s, v_s)
```
